```python
import jax, jax.numpy as jnp
from jax import lax
import numpy as np

D_MODEL = 1024
BATCH = 1
SEQ = 16384
DEPTH = 1
DEC_BATCH = 32
DEC_SEQ = 1
PAST_LEN = 16384
PAGE_SIZE = 128

D_MIX = D_MODEL
D_ATT = D_MIX // 2
D_GM = D_MIX - D_ATT
HEAD_DIM = 64
N_HEADS = D_ATT // HEAD_DIM
GM_GROUP_DIM = 64
N_GM_GROUPS = D_GM // GM_GROUP_DIM
CHUNK = 128
IDX_HEADS = 8
IDX_DIM = 64
TOPK_MAX = 256
ROPE_THETA = 500000.0
ROPE_FRAC = 4
D_FF = ((8 * D_MODEL // 3 + 255) // 256) * 256
CONV_W = 3
LN_EPS = 1e-5
DEEPNORM_ALPHA = (2.0 * DEPTH) ** 0.25
DEEPNORM_BETA = (8.0 * DEPTH) ** -0.25
Q_BLOCK = 128
OFF_Q = 0
OFF_K = OFF_Q + D_ATT
OFF_V = OFF_K + D_ATT
OFF_QI = OFF_V + D_ATT
OFF_KI = OFF_QI + IDX_HEADS * IDX_DIM
OFF_WI = OFF_KI + IDX_DIM
OFF_U = OFF_WI + IDX_HEADS
OFF_GV = OFF_U + D_GM
D_IN = OFF_GV + D_GM

kernel_name = "hybrid_dsa_gmlp_convffn_decoder_step"


def layer_norm(x, g, b):
    xf = x.astype(jnp.float32)
    mu = jnp.mean(xf, -1, keepdims=True)
    var = jnp.mean(jnp.square(xf - mu), -1, keepdims=True)
    return ((xf - mu) * lax.rsqrt(var + LN_EPS) * g.astype(jnp.float32) + b.astype(jnp.float32)).astype(x.dtype)


def apply_rope(x, pos):
    dh = x.shape[-1]
    rot = dh // ROPE_FRAC
    half = rot // 2
    inv = ROPE_THETA ** (-jnp.arange(half, dtype=jnp.float32) / half)
    ang = pos.astype(jnp.float32)[:, None] * inv[None, :]
    cos = jnp.cos(ang)[None, :, None, :].astype(x.dtype)
    sin = jnp.sin(ang)[None, :, None, :].astype(x.dtype)
    x1 = x[..., :half]
    x2 = x[..., half:rot]
    return jnp.concatenate([x1 * cos - x2 * sin, x1 * sin + x2 * cos, x[..., rot:]], -1)


def project_mix(h, pos, w_in):
    B, T = h.shape[:2]
    z = h @ w_in
    q = apply_rope(z[..., OFF_Q:OFF_K].reshape(B, T, N_HEADS, HEAD_DIM), pos)
    k = apply_rope(z[..., OFF_K:OFF_V].reshape(B, T, N_HEADS, HEAD_DIM), pos)
    v = z[..., OFF_V:OFF_QI].reshape(B, T, N_HEADS, HEAD_DIM)
    qi = apply_rope(z[..., OFF_QI:OFF_KI].reshape(B, T, IDX_HEADS, IDX_DIM), pos)
    ki = apply_rope(z[..., OFF_KI:OFF_WI].reshape(B, T, 1, IDX_DIM), pos)[:, :, 0]
    wi = z[..., OFF_WI:OFF_U] * (IDX_HEADS ** -0.5)
    u = jax.nn.gelu(z[..., OFF_U:OFF_GV])
    gv = jax.nn.gelu(z[..., OFF_GV:])
    return q, k, v, qi, ki, wi, u, gv


def indexer_select(qi, wi, ki, q_pos, n_sel):
    s = jnp.einsum('bqhd,bsd->bqhs', qi.astype(jnp.float32), ki.astype(jnp.float32)) * (IDX_DIM ** -0.5)
    score = jnp.einsum('bqhs,bqh->bqs', jax.nn.relu(s), wi.astype(jnp.float32))
    key_pos = jnp.arange(ki.shape[1])
    admissible = key_pos[None, :] <= q_pos[:, None]
    score = jnp.where(admissible[None], score, -jnp.inf)
    _, idx = lax.top_k(score, n_sel)
    valid = idx <= q_pos[None, :, None]
    return idx, valid


def sparse_attend(q, k_sel, v_sel, valid):
    logits = jnp.einsum('bqhd,bqkhd->bqhk', q.astype(jnp.float32), k_sel.astype(jnp.float32)) * (HEAD_DIM ** -0.5)
    logits = jnp.where(valid[:, :, None, :], logits, -jnp.inf)
    p = jax.nn.softmax(logits, axis=-1)
    return jnp.einsum('bqhk,bqkhd->bqhd', p, v_sel.astype(jnp.float32)).astype(q.dtype)


def gather_rows(rows, ids):
    return jax.vmap(lambda r, i: r[i])(rows, ids)


def prompt_sparse_attention(q, k, v, qi, wi, ki):
    B, T = q.shape[:2]
    n_sel = min(TOPK_MAX, T // 4)
    nb = T // Q_BLOCK

    def to_blocks(a):
        return a.reshape((B, nb, Q_BLOCK) + a.shape[2:]).swapaxes(0, 1)

    def block(args):
        qb, qib, wib, b_id = args
        q_pos = b_id * Q_BLOCK + jnp.arange(Q_BLOCK)
        idx, valid = indexer_select(qib, wib, ki, q_pos, n_sel)
        return sparse_attend(qb, gather_rows(k, idx), gather_rows(v, idx), valid)

    out = lax.map(block, (to_blocks(q), to_blocks(qi), to_blocks(wi), jnp.arange(nb)))
    return out.swapaxes(0, 1).reshape(B, T, N_HEADS, HEAD_DIM)


def sample_sparse_attention(q, k_new, v_new, qi, wi, ki_new, cache_k, cache_v, cache_idx_k, page_table):
    DB, T = q.shape[:2]
    n_pages = page_table.shape[1]
    past = n_pages * PAGE_SIZE
    n_sel = min(TOPK_MAX, (past + T) // 4)
    ki_past = cache_idx_k[page_table].reshape(DB, past, IDX_DIM)
    ki_all = jnp.concatenate([ki_past, ki_new.astype(ki_past.dtype)], axis=1)
    q_pos = past + jnp.arange(T)
    idx, valid = indexer_select(qi, wi, ki_all, q_pos, n_sel)
    is_new = idx >= past
    pidx = jnp.minimum(idx, past - 1)
    phys = jnp.take_along_axis(page_table, (pidx // PAGE_SIZE).reshape(DB, -1), axis=1).reshape(pidx.shape)
    off = pidx % PAGE_SIZE
    nidx = jnp.clip(idx - past, 0, T - 1)

    def select(cache, new):
        return jnp.where(is_new[..., None, None], gather_rows(new, nidx), cache[phys, off].astype(new.dtype))

    return sparse_attend(q, select(cache_k, k_new), select(cache_v, v_new), valid)


def chunk_gmlp(u, gv, gm_ln_g, gm_ln_b, gm_ws, gm_bs):
    B, T = u.shape[:2]
    vn = layer_norm(gv, gm_ln_g, gm_ln_b)
    nc = -(-T // CHUNK)
    pad = nc * CHUNK - T
    vp = jnp.pad(vn, ((0, 0), (0, pad), (0, 0))).reshape(B, nc, CHUNK, N_GM_GROUPS, GM_GROUP_DIM)
    w_causal = gm_ws * jnp.tril(jnp.ones((CHUNK, CHUNK), gm_ws.dtype))
    mixed = jnp.einsum('gts,bcsgd->bctgd', w_causal, vp) + gm_bs.T[None, None, :, :, None]
    mixed = mixed.reshape(B, nc * CHUNK, D_GM)[:, :T]
    return u * mixed, vn


def conv_ffn(h, prev, w_up, conv_w, conv_b, w_down):
    T = h.shape[1]
    up = h @ w_up
    g, a = up[..., :D_FF], up[..., D_FF:]
    g_ext = jnp.concatenate([prev.astype(g.dtype), g], axis=1)
    g_conv = conv_b
    for j in range(CONV_W):
        g_conv = g_conv + conv_w[j] * g_ext[:, j:j + T]
    f = (jax.nn.silu(g_conv) * a) @ w_down
    return f, g_ext[:, -(CONV_W - 1):]


def decoder_layer(x, c, pos, conv_prev, attention, w_ada, b_ada, w_in, gm_ln_g, gm_ln_b, gm_ws, gm_bs,
                  w_out, ln1_g, ln1_b, w_up, conv_w, conv_b, w_down, ln2_g, ln2_b):
    B, T = x.shape[:2]
    mod = jax.nn.silu(c) @ w_ada + b_ada
    sh1, sc1, g1, sh2, sc2, g2 = jnp.split(mod[:, None, :], 6, axis=-1)
    h = x * (1 + sc1) + sh1
    q, k, v, qi, ki, wi, u, gv = project_mix(h, pos, w_in)
    att = attention(q, k, v, qi, wi, ki).reshape(B, T, D_ATT)
    gm, vn = chunk_gmlp(u, gv, gm_ln_g, gm_ln_b, gm_ws, gm_bs)
    mix = jnp.concatenate([att, gm], axis=-1) @ w_out
    x = layer_norm(DEEPNORM_ALPHA * x + g1 * mix, ln1_g, ln1_b)
    h2 = x * (1 + sc2) + sh2
    f, conv_new = conv_ffn(h2, conv_prev, w_up, conv_w, conv_b, w_down)
    y = layer_norm(DEEPNORM_ALPHA * x + g2 * f, ln2_g, ln2_b)
    return y, k, v, ki, vn, conv_new


def setup_inputs(seed: int = 0) -> dict:
    key = jax.random.key(seed)
    ks = jax.random.split(key, 32)
    n_pages = PAST_LEN // PAGE_SIZE
    n_used = DEC_BATCH * n_pages
    n_phys = n_used + max(1, n_used // 4)
    nrm = jax.random.normal
    f32 = jnp.float32
    page_table = jax.random.permutation(ks[0], n_phys)[:n_used].reshape(DEC_BATCH, n_pages).astype(jnp.int32)
    return {
        "x_prompt": nrm(ks[1], (BATCH, SEQ, D_MODEL), f32),
        "x_sample": nrm(ks[2], (DEC_BATCH, DEC_SEQ, D_MODEL), f32),
        "c_prompt": nrm(ks[3], (BATCH, D_MODEL), f32),
        "c_sample": nrm(ks[4], (DEC_BATCH, D_MODEL), f32),
        "cache_k": nrm(ks[5], (DEPTH, n_phys, PAGE_SIZE, N_HEADS, HEAD_DIM), f32),
        "cache_v": nrm(ks[6], (DEPTH, n_phys, PAGE_SIZE, N_HEADS, HEAD_DIM), f32),
        "cache_idx_k": nrm(ks[7], (DEPTH, n_phys, PAGE_SIZE, IDX_DIM), f32),
        "state_conv": nrm(ks[8], (DEPTH, DEC_BATCH, CONV_W - 1, D_FF), f32) * 0.5,
        "page_table": page_table,
        "w_ada": nrm(ks[9], (DEPTH, D_MODEL, 6 * D_MODEL), f32) * (0.5 * D_MODEL ** -0.5),
        "b_ada": nrm(ks[10], (DEPTH, 6 * D_MODEL), f32) * 0.01,
        "w_in": nrm(ks[11], (DEPTH, D_MODEL, D_IN), f32) * (D_MODEL ** -0.5),
        "gm_ln_g": 1.0 + 0.01 * nrm(ks[12], (DEPTH, D_GM), f32),
        "gm_ln_b": 0.01 * nrm(ks[13], (DEPTH, D_GM), f32),
        "gm_ws": nrm(ks[14], (DEPTH, N_GM_GROUPS, CHUNK, CHUNK), f32) * (CHUNK ** -0.5),
        "gm_bs": 1.0 + 0.01 * nrm(ks[15], (DEPTH, N_GM_GROUPS, CHUNK), f32),
        "w_out": nrm(ks[16], (DEPTH, D_MIX, D_MODEL), f32) * (D_MIX ** -0.5) * DEEPNORM_BETA,
        "ln1_g": 1.0 + 0.01 * nrm(ks[17], (DEPTH, D_MODEL), f32),
        "ln1_b": 0.01 * nrm(ks[18], (DEPTH, D_MODEL), f32),
        "w_up": nrm(ks[19], (DEPTH, D_MODEL, 2 * D_FF), f32) * (D_MODEL ** -0.5),
        "conv_w": nrm(ks[20], (DEPTH, CONV_W, D_FF), f32) * (CONV_W ** -0.5),
        "conv_b": 0.01 * nrm(ks[21], (DEPTH, D_FF), f32),
        "w_down": nrm(ks[22], (DEPTH, D_FF, D_MODEL), f32) * (D_FF ** -0.5) * DEEPNORM_BETA,
        "ln2_g": 1.0 + 0.01 * nrm(ks[23], (DEPTH, D_MODEL), f32),
        "ln2_b": 0.01 * nrm(ks[24], (DEPTH, D_MODEL), f32),
    }


def reference(x_prompt, x_sample, c_prompt, c_sample, cache_k, cache_v, cache_idx_k, state_conv, page_table,
              w_ada, b_ada, w_in, gm_ln_g, gm_ln_b, gm_ws, gm_bs, w_out, ln1_g, ln1_b,
              w_up, conv_w, conv_b, w_down, ln2_g, ln2_b):
    B, T = x_prompt.shape[:2]
    DB, TS = x_sample.shape[:2]
    past = page_table.shape[1] * PAGE_SIZE
    pos_prompt = jnp.arange(T)
    pos_sample = past + jnp.arange(TS)
    conv_zero = jnp.zeros((B, CONV_W - 1, D_FF), x_prompt.dtype)
    yp, ys = x_prompt, x_sample
    kp_l, vp_l, ip_l, cp_l = [], [], [], []
    ks_l, vs_l, is_l, gs_l, cs_l = [], [], [], [], []
    for l in range(DEPTH):
        weights = (w_ada[l], b_ada[l], w_in[l], gm_ln_g[l], gm_ln_b[l], gm_ws[l], gm_bs[l], w_out[l],
                   ln1_g[l], ln1_b[l], w_up[l], conv_w[l], conv_b[l], w_down[l], ln2_g[l], ln2_b[l])
        ck, cv, cik = cache_k[l], cache_v[l], cache_idx_k[l]

        def sample_attention(q, k, v, qi, wi, ki, ck=ck, cv=cv, cik=cik):
            return sample_sparse_attention(q, k, v, qi, wi, ki, ck, cv, cik, page_table)

        yp, kp, vp, ip, _, cp = decoder_layer(yp, c_prompt, pos_prompt, conv_zero, prompt_sparse_attention, *weights)
        ys, ks_, vs_, is_, gs, cs = decoder_layer(ys, c_sample, pos_sample, state_conv[l], sample_attention, *weights)
        kp_l.append(kp); vp_l.append(vp); ip_l.append(ip); cp_l.append(cp)
        ks_l.append(ks_); vs_l.append(vs_); is_l.append(is_); gs_l.append(gs); cs_l.append(cs)
    return (yp, ys, jnp.stack(kp_l), jnp.stack(vp_l), jnp.stack(ip_l), jnp.stack(cp_l),
            jnp.stack(ks_l), jnp.stack(vs_l), jnp.stack(is_l), jnp.stack(gs_l), jnp.stack(cs_l))
```

```python
import functools

import jax
import jax.numpy as jnp
from jax import lax
from jax.experimental import pallas as pl
from jax.experimental.pallas import tpu as pltpu

F32 = jnp.float32
BF16 = jnp.bfloat16
I32 = jnp.int32

HEAD_DIM = 64
IDX_HEADS = 8
IDX_DIM = 64
TOPK_MAX = 256
ROPE_THETA = 500000.0
ROPE_FRAC = 4
GM_GROUP_DIM = 64
CHUNK = 128
CONV_W = 3
LN_EPS = 1e-5

LANES = 128
INT_MIN = -(2 ** 31)
NEG = -1e30
VMEM_LIMIT = 56 * 1024 * 1024

C_Q, C_K, C_V, C_QI, C_KW, C_U, C_GV, C_END = 0, 512, 1024, 1536, 2048, 2176, 2688, 3200


def _vmem_full():
    return pl.BlockSpec(memory_space=pltpu.VMEM)


def _params(sem):
    return pltpu.CompilerParams(dimension_semantics=sem, vmem_limit_bytes=VMEM_LIMIT)


def _layer_norm(x, g, b):
    mu = jnp.mean(x, axis=-1, keepdims=True)
    xc = x - mu
    var = jnp.mean(xc * xc, axis=-1, keepdims=True)
    return xc * lax.rsqrt(var + LN_EPS) * g + b


def _low_half(rows):
    return lax.broadcasted_iota(I32, (rows, LANES), 1) < HEAD_DIM


def _mod_kernel(c_ref, w_ref, b_ref, o_ref):
    c = c_ref[...]
    s = c * jax.nn.sigmoid(c)
    o_ref[...] = jnp.dot(s.astype(BF16), w_ref[...], preferred_element_type=F32) + b_ref[...]


def _modulation(c, w_ada, b_ada):
    rows, d = c.shape
    n = w_ada.shape[1]
    tn = n // 4
    return pl.pallas_call(
        _mod_kernel,
        grid=(4,),
        in_specs=[pl.BlockSpec((rows, d), lambda j: (0, 0)),
                  pl.BlockSpec((d, tn), lambda j: (0, j)),
                  pl.BlockSpec((1, tn), lambda j: (0, j))],
        out_specs=pl.BlockSpec((rows, tn), lambda j: (0, j)),
        out_shape=jax.ShapeDtypeStruct((rows, n), F32),
        compiler_params=_params(("arbitrary",)),
        name="modulation",
    )(c, w_ada, b_ada)


def _proj_kernel(x_ref, sc_ref, sh_ref, w_ref, rope_ref, lng_ref, lnb_ref,
                 q_ref, kf_ref, vf_ref, kb_ref, vb_ref, qi_ref, kw_ref, u_ref, vn_ref):
    tm = x_ref.shape[0]
    h = x_ref[...] * (1.0 + sc_ref[...]) + sh_ref[...]
    hb = h.astype(BF16)
    cos = rope_ref[:, 0:LANES]
    sin_a = rope_ref[:, LANES:2 * LANES]
    sin_b = rope_ref[:, 2 * LANES:3 * LANES]

    def rope(zc):
        return zc * cos + pltpu.roll(zc, LANES - 8, 1) * sin_a + pltpu.roll(zc, 8, 1) * sin_b

    def mm(c0, c1):
        return jnp.dot(hb, w_ref[:, c0:c1], preferred_element_type=F32)

    zq = mm(C_Q, C_K)
    zk = mm(C_K, C_V)
    zv = mm(C_V, C_QI)
    zqi = mm(C_QI, C_KW)
    for j in range(4):
        sl = slice(j * LANES, (j + 1) * LANES)
        q_ref[:, sl] = (rope(zq[:, sl]) * (HEAD_DIM ** -0.5)).astype(BF16)
        kr = rope(zk[:, sl])
        kf_ref[:, sl] = kr
        kb_ref[:, sl] = kr.astype(BF16)
        qi_ref[:, sl] = (rope(zqi[:, sl]) * (IDX_DIM ** -0.5)).astype(BF16)
    vf_ref[...] = zv
    vb_ref[...] = zv.astype(BF16)
    zkw = mm(C_KW, C_U)
    kw_ref[...] = jnp.where(_low_half(tm), rope(zkw), zkw * (IDX_HEADS ** -0.5))
    u_ref[...] = jax.nn.gelu(mm(C_U, C_GV))
    gv = jax.nn.gelu(mm(C_GV, C_END))
    vn_ref[...] = _layer_norm(gv, lng_ref[...], lnb_ref[...])


def _project(x, sc, sh, w_in_r, rope_tab, ln_g, ln_b, tm):
    rows, d = x.shape
    mod_rows = sc.shape[0]
    tmm = tm if mod_rows == rows else 1
    mod_map = (lambda i: (i, 0)) if mod_rows == rows else (lambda i: (0, 0))
    row = lambda w: pl.BlockSpec((tm, w), lambda i: (i, 0))
    const = lambda a: pl.BlockSpec(a.shape, lambda i: (0, 0))
    dh = 512
    out_shape = (jax.ShapeDtypeStruct((rows, dh), BF16),
                 jax.ShapeDtypeStruct((rows, dh), F32),
                 jax.ShapeDtypeStruct((rows, dh), F32),
                 jax.ShapeDtypeStruct((rows, dh), BF16),
                 jax.ShapeDtypeStruct((rows, dh), BF16),
                 jax.ShapeDtypeStruct((rows, dh), BF16),
                 jax.ShapeDtypeStruct((rows, LANES), F32),
                 jax.ShapeDtypeStruct((rows, dh), F32),
                 jax.ShapeDtypeStruct((rows, dh), F32))
    return pl.pallas_call(
        _proj_kernel,
        grid=(rows // tm,),
        in_specs=[row(d),
                  pl.BlockSpec((tmm, d), mod_map), pl.BlockSpec((tmm, d), mod_map),
                  _vmem_full(), row(3 * LANES), const(ln_g), const(ln_b)],
        out_specs=[row(dh), row(dh), row(dh), row(dh), row(dh), row(dh), row(LANES), row(dh), row(dh)],
        out_shape=out_shape,
        compiler_params=_params(("arbitrary",)),
        name="projection",
    )(x, sc, sh, w_in_r, rope_tab, ln_g, ln_b)


def _ordered_key(score):
    bits = lax.bitcast_convert_type(score, I32)
    return bits ^ ((bits >> 31) & 0x7FFFFFFF)


def _count(keys_ref, n_chunks, rows, pred):
    def body(c, acc):
        return acc + jnp.where(pred(keys_ref[c], c), 1.0, 0.0)
    acc = lax.fori_loop(0, n_chunks, body, jnp.zeros((rows, LANES), F32))
    return jnp.sum(acc, axis=1, keepdims=True)


def _topk_threshold(keys_ref, n_chunks, rows, n_sel, idx_bits):
    def value_pass(i, t):
        cand = t + jnp.left_shift(jnp.int32(1), 31 - i)
        cand_b = jnp.broadcast_to(cand, (rows, LANES))
        cnt = _count(keys_ref, n_chunks, rows, lambda k, c: k >= cand_b)
        return jnp.where(cnt >= n_sel, cand, t)

    t = lax.fori_loop(0, 32, value_pass, jnp.full((rows, 1), INT_MIN, I32))
    t_b = jnp.broadcast_to(t, (rows, LANES))
    cnt_ge = _count(keys_ref, n_chunks, rows, lambda k, c: k >= t_b)
    tied = jnp.logical_and(cnt_ge > n_sel, t > INT_MIN)
    any_tied = jnp.max(jnp.where(tied, 1.0, 0.0)) > 0.0

    @pl.when(any_tied)
    def _():
        cnt_gt = _count(keys_ref, n_chunks, rows, lambda k, c: k > t_b)
        need = n_sel - cnt_gt
        lane = lax.broadcasted_iota(I32, (rows, LANES), 1)

        def index_pass(i, j):
            cand = j + jnp.left_shift(jnp.int32(1), idx_bits - 1 - i)
            cand_b = jnp.broadcast_to(cand, (rows, LANES))
            cnt = _count(keys_ref, n_chunks, rows,
                         lambda k, c: jnp.where(k == t_b, c * LANES + lane, cand_b) < cand_b)
            return jnp.where(cnt < need, cand, j)

        j = lax.fori_loop(0, idx_bits, index_pass, jnp.zeros((rows, 1), I32))
        j_b = jnp.broadcast_to(jnp.where(tied, j, jnp.int32(2 ** 31 - 1)), (rows, LANES))

        def drop(c, carry):
            k = keys_ref[c]
            over = jnp.where(k == t_b, c * LANES + lane, j_b) > j_b
            keys_ref[c] = jnp.where(over, k - 1, k)
            return carry
        lax.fori_loop(0, n_chunks, drop, 0)

    return jnp.maximum(t, INT_MIN + 1)


def _prompt_attn_kernel(q_ref, qi_ref, kw_ref, kt_ref, v_ref, kit_ref, o_ref,
                        keys_ref, bias_ref, qm_ref, qim_ref, wb_ref, m_ref, l_ref, acc_ref,
                        *, tq, tkb, n_sel, idx_bits):
    qb = pl.program_id(0)
    q0 = qb * tq
    n_kb = (q0 + tq - 1) // tkb + 1
    cpb = tkb // LANES
    low = _low_half(tq)
    lane = lax.broadcasted_iota(I32, (tq, LANES), 1)
    zero_b = jnp.zeros((tq, LANES), BF16)

    for j in range(4):
        sl = slice(j * LANES, (j + 1) * LANES)
        qc = q_ref[:, sl]
        qm_ref[2 * j] = jnp.where(low, qc, zero_b)
        qm_ref[2 * j + 1] = jnp.where(low, zero_b, qc)
        qic = qi_ref[:, sl]
        qim_ref[2 * j] = jnp.where(low, qic, zero_b)
        qim_ref[2 * j + 1] = jnp.where(low, zero_b, qic)
    kw = kw_ref[...]
    for h in range(IDX_HEADS):
        w_h = jnp.sum(jnp.where(lane == IDX_DIM + h, kw, 0.0), axis=1, keepdims=True)
        wb_ref[h] = jnp.broadcast_to(w_h, (tq, LANES))

    q_pos = q0 + lax.broadcasted_iota(I32, (tq, LANES), 0)

    def score_block(kb, carry):
        kit = kit_ref[kb]
        acc = [jnp.zeros((tq, LANES), F32) for _ in range(cpb)]
        for h in range(IDX_HEADS):
            s = jnp.dot(qim_ref[h], kit, preferred_element_type=F32)
            w_h = wb_ref[h]
            for c in range(cpb):
                acc[c] = acc[c] + jnp.maximum(s[:, c * LANES:(c + 1) * LANES], 0.0) * w_h
        for c in range(cpb):
            key_pos = kb * tkb + c * LANES + lane
            keys_ref[kb * cpb + c] = jnp.where(key_pos <= q_pos, _ordered_key(acc[c]), INT_MIN)
        return carry

    lax.fori_loop(0, n_kb, score_block, 0)

    thr = _topk_threshold(keys_ref, n_kb * cpb, tq, n_sel, idx_bits)
    thr_b = jnp.broadcast_to(thr, (tq, LANES))

    m_ref[...] = jnp.full(m_ref.shape, NEG, F32)
    l_ref[...] = jnp.zeros(l_ref.shape, F32)
    acc_ref[...] = jnp.zeros(acc_ref.shape, F32)

    def attend_block(kb, carry):
        for c in range(cpb):
            bias_ref[:, c * LANES:(c + 1) * LANES] = jnp.where(keys_ref[kb * cpb + c] >= thr_b, 0.0, NEG)
        for j in range(4):
            kt2 = kt_ref[kb, j * LANES:(j + 1) * LANES, :]
            v2 = v_ref[kb, :, j * LANES:(j + 1) * LANES]
            upd = []
            for e in range(2):
                h = 2 * j + e
                s = jnp.dot(qm_ref[h], kt2, preferred_element_type=F32) + bias_ref[...]
                m_prev = m_ref[h]
                m_new = jnp.maximum(m_prev, jnp.max(s, axis=1, keepdims=True))
                alpha = jnp.exp(m_prev - m_new)
                p = jnp.exp(s - m_new)
                l_ref[h] = alpha * l_ref[h] + jnp.sum(p, axis=1, keepdims=True)
                m_ref[h] = m_new
                upd.append((alpha, jnp.dot(p.astype(BF16), v2, preferred_element_type=F32)))
            a = acc_ref[j]
            acc_ref[j] = jnp.where(low, upd[0][0] * a + upd[0][1], upd[1][0] * a + upd[1][1])
        return carry

    lax.fori_loop(0, n_kb, attend_block, 0)

    for j in range(4):
        inv = jnp.where(low, 1.0 / l_ref[2 * j], 1.0 / l_ref[2 * j + 1])
        o_ref[:, j * LANES:(j + 1) * LANES] = (acc_ref[j] * inv).astype(o_ref.dtype)


def _prompt_attention(q_bf, qi_bf, kw, k_bf, v_bf, n_sel):
    t, dh = q_bf.shape
    tq, tkb = 128, 512
    nb = t // tkb
    n_heads = dh // HEAD_DIM
    kt3 = k_bf.reshape(nb, tkb, dh).transpose(0, 2, 1)
    v3 = v_bf.reshape(nb, tkb, dh)
    kit = kw[:, :IDX_DIM].astype(BF16).reshape(nb, tkb, IDX_DIM).transpose(0, 2, 1)
    kit2 = jnp.concatenate([kit, kit], axis=1)
    kern = functools.partial(_prompt_attn_kernel, tq=tq, tkb=tkb, n_sel=n_sel,
                             idx_bits=max(1, (t - 1).bit_length()))
    row = lambda w: pl.BlockSpec((tq, w), lambda i: (i, 0))
    return pl.pallas_call(
        kern,
        grid=(t // tq,),
        in_specs=[row(dh), row(dh), row(LANES), _vmem_full(), _vmem_full(), _vmem_full()],
        out_specs=row(dh),
        out_shape=jax.ShapeDtypeStruct((t, dh), BF16),
        scratch_shapes=[pltpu.VMEM((t // LANES, tq, LANES), I32),
                        pltpu.VMEM((tq, tkb), F32),
                        pltpu.VMEM((n_heads, tq, LANES), BF16),
                        pltpu.VMEM((IDX_HEADS, tq, LANES), BF16),
                        pltpu.VMEM((IDX_HEADS, tq, LANES), F32),
                        pltpu.VMEM((n_heads, tq, 1), F32),
                        pltpu.VMEM((n_heads, tq, 1), F32),
                        pltpu.VMEM((n_heads // 2, tq, LANES), F32)],
        compiler_params=_params(("arbitrary",)),
        name="prompt_attention",
    )(q_bf, qi_bf, kw, kt3, v3, kit2)


def _gmlp_gate(u, vn, wp_ref, gb_ref, low):
    cols = []
    for j in range(4):
        sl = slice(j * LANES, (j + 1) * LANES)
        col = vn[:, sl]
        rhs = jnp.concatenate([jnp.where(low, col, 0.0), jnp.where(low, 0.0, col)], axis=0).astype(BF16)
        mixed = jnp.dot(wp_ref[j], rhs, preferred_element_type=F32) + gb_ref[:, sl]
        cols.append(u[:, sl] * mixed)
    return jnp.concatenate(cols, axis=1)


def _prompt_mix_kernel(x_ref, att_ref, u_ref, vn_ref, g1_ref, sc2_ref, sh2_ref, g2_ref,
                       wp_ref, gb_ref, wout_ref, ln1g_ref, ln1b_ref,
                       wup_ref, cw_ref, cb_ref, wdown_ref, ln2g_ref, ln2b_ref,
                       y_ref, conv_ref, gbuf_ref, *, alpha, ffc):
    tm = x_ref.shape[0]
    d_att = att_ref.shape[1]
    d_ff = cw_ref.shape[1]
    low = _low_half(CHUNK)

    @pl.when(pl.program_id(0) == 0)
    def _():
        gbuf_ref[0:8, :] = jnp.zeros((8, d_ff), F32)

    gm = jnp.concatenate(
        [_gmlp_gate(u_ref[c * CHUNK:(c + 1) * CHUNK, :], vn_ref[c * CHUNK:(c + 1) * CHUNK, :], wp_ref, gb_ref, low)
         for c in range(tm // CHUNK)], axis=0)
    mix = (jnp.dot(att_ref[...], wout_ref[0:d_att, :], preferred_element_type=F32)
           + jnp.dot(gm.astype(BF16), wout_ref[d_att:, :], preferred_element_type=F32))
    x1 = _layer_norm(alpha * x_ref[...] + g1_ref[...] * mix, ln1g_ref[...], ln1b_ref[...])
    h2b = (x1 * (1.0 + sc2_ref[...]) + sh2_ref[...]).astype(BF16)

    f = jnp.zeros((tm, x_ref.shape[1]), F32)
    for c in range(d_ff // ffc):
        sl = slice(c * ffc, (c + 1) * ffc)
        g = jnp.dot(h2b, wup_ref[:, sl], preferred_element_type=F32)
        a = jnp.dot(h2b, wup_ref[:, d_ff + c * ffc:d_ff + (c + 1) * ffc], preferred_element_type=F32)
        gbuf_ref[8:8 + tm, sl] = g
        g_conv = (cb_ref[:, sl] + cw_ref[0:1, sl] * gbuf_ref[6:6 + tm, sl]
                  + cw_ref[1:2, sl] * gbuf_ref[7:7 + tm, sl] + cw_ref[2:3, sl] * g)
        act = g_conv * jax.nn.sigmoid(g_conv) * a
        f = f + jnp.dot(act.astype(BF16), wdown_ref[sl, :], preferred_element_type=F32)
        gbuf_ref[6:8, sl] = gbuf_ref[tm + 6:tm + 8, sl]
    y_ref[...] = _layer_norm(alpha * x1 + g2_ref[...] * f, ln2g_ref[...], ln2b_ref[...])
    conv_ref[...] = gbuf_ref[6:8, :]


def _prompt_mix_ffn(x, att, u, vn, g1, sc2, sh2, g2, wpair, gbias, w_out, ln1g, ln1b,
                    w_up, conv_w, conv_b, w_down, ln2g, ln2b, alpha):
    t, d = x.shape
    tm = 256
    d_ff = conv_w.shape[1]
    row = lambda w: pl.BlockSpec((tm, w), lambda i: (i, 0))
    const = lambda a: pl.BlockSpec(a.shape, lambda i: (0,) * a.ndim)
    kern = functools.partial(_prompt_mix_kernel, alpha=alpha, ffc=256)
    return pl.pallas_call(
        kern,
        grid=(t // tm,),
        in_specs=[row(d), row(att.shape[1]), row(u.shape[1]), row(vn.shape[1]),
                  const(g1), const(sc2), const(sh2), const(g2),
                  _vmem_full(), const(gbias), _vmem_full(), const(ln1g), const(ln1b),
                  _vmem_full(), const(conv_w), const(conv_b), _vmem_full(), const(ln2g), const(ln2b)],
        out_specs=[row(d), pl.BlockSpec((CONV_W - 1, d_ff), lambda i: (0, 0))],
        out_shape=(jax.ShapeDtypeStruct((t, d), F32), jax.ShapeDtypeStruct((CONV_W - 1, d_ff), F32)),
        scratch_shapes=[pltpu.VMEM((tm + 8, d_ff), F32)],
        compiler_params=_params(("arbitrary",)),
        name="prompt_mix_ffn",
    )(x, att, u, vn, g1, sc2, sh2, g2, wpair, gbias, w_out, ln1g, ln1b,
      w_up, conv_w, conv_b, w_down, ln2g, ln2b)


def _sample_score_kernel(pt_ref, q8_ref, w8_ref, *refs, pages):
    page_refs, o_ref = refs[:pages], refs[pages]
    q8 = q8_ref[0]
    w8 = w8_ref[0]
    for i in range(pages):
        ki = page_refs[i][0].astype(BF16)
        s = lax.dot_general(q8, ki, (((1,), (1,)), ((), ())), preferred_element_type=F32)
        sc = jnp.sum(jnp.maximum(s, 0.0) * w8, axis=0, keepdims=True)
        o_ref[0, :, i * ki.shape[0]:(i + 1) * ki.shape[0]] = sc


def _sample_scores(page_table, q8, w8, cache_idx_k, pages):
    db, n_pages = page_table.shape
    _, page, di = cache_idx_k.shape
    page_spec = lambda i: pl.BlockSpec((1, page, di), lambda b, g, pt: (pt[b, g * pages + i], 0, 0))
    grid_spec = pltpu.PrefetchScalarGridSpec(
        num_scalar_prefetch=1,
        grid=(db, n_pages // pages),
        in_specs=[pl.BlockSpec((1, IDX_HEADS, di), lambda b, g, pt: (b, 0, 0)),
                  pl.BlockSpec((1, IDX_HEADS, 1), lambda b, g, pt: (b, 0, 0))]
                 + [page_spec(i) for i in range(pages)],
        out_specs=pl.BlockSpec((1, 1, pages * page), lambda b, g, pt: (b, 0, g)),
    )
    return pl.pallas_call(
        functools.partial(_sample_score_kernel, pages=pages),
        grid_spec=grid_spec,
        out_shape=jax.ShapeDtypeStruct((db, 1, n_pages * page), F32),
        compiler_params=_params(("arbitrary", "arbitrary")),
        name="sample_scores",
    )(page_table, q8, w8, *([cache_idx_k] * pages))


def _sample_select_kernel(sc_ref, qi_ref, kw_ref, bp_ref, bn_ref, keys_ref, *, n_sel, idx_bits):
    db, past = sc_ref.shape
    n_past = past // LANES
    low = _low_half(db)
    lane = lax.broadcasted_iota(I32, (db, LANES), 1)
    kw = kw_ref[...]
    ki2 = jnp.where(low, kw, pltpu.roll(kw, HEAD_DIM, 1))
    s_new = jnp.zeros((db, 1), F32)
    for j in range(4):
        prod = qi_ref[:, j * LANES:(j + 1) * LANES] * ki2
        for e in range(2):
            h = 2 * j + e
            s_h = jnp.sum(jnp.where(low, prod, 0.0) if e == 0 else jnp.where(low, 0.0, prod), axis=1, keepdims=True)
            w_h = jnp.sum(jnp.where(lane == IDX_DIM + h, kw, 0.0), axis=1, keepdims=True)
            s_new = s_new + jnp.maximum(s_h, 0.0) * w_h
    for c in range(n_past):
        keys_ref[c] = _ordered_key(sc_ref[:, c * LANES:(c + 1) * LANES])
    keys_ref[n_past] = jnp.where(lane == 0, _ordered_key(jnp.broadcast_to(s_new, (db, LANES))), INT_MIN)
    thr = _topk_threshold(keys_ref, n_past + 1, db, n_sel, idx_bits)
    thr_b = jnp.broadcast_to(thr, (db, LANES))
    for c in range(n_past):
        bp_ref[:, c * LANES:(c + 1) * LANES] = jnp.where(keys_ref[c] >= thr_b, 0.0, NEG)
    bn_ref[...] = jnp.where(keys_ref[n_past] >= thr_b, 0.0, NEG)


def _sample_select(scores, qi_f, kw, n_sel):
    db, past = scores.shape
    kern = functools.partial(_sample_select_kernel, n_sel=n_sel, idx_bits=past.bit_length())
    return pl.pallas_call(
        kern,
        in_specs=[_vmem_full(), _vmem_full(), _vmem_full()],
        out_specs=[_vmem_full(), _vmem_full()],
        out_shape=(jax.ShapeDtypeStruct((db, past), F32), jax.ShapeDtypeStruct((db, LANES), F32)),
        scratch_shapes=[pltpu.VMEM((past // LANES + 1, db, LANES), I32)],
        compiler_params=pltpu.CompilerParams(vmem_limit_bytes=VMEM_LIMIT),
        name="sample_select",
    )(scores, qi_f, kw)


def _sample_attn_kernel(pt_ref, qd_ref, bp_ref, bn_ref, kn_ref, vn_ref, *refs, pages):
    k_refs, v_refs = refs[:pages], refs[pages:2 * pages]
    o_ref, m_ref, l_ref, acc_ref = refs[2 * pages:]
    g = pl.program_id(1)
    n_heads, dh = qd_ref.shape[1], qd_ref.shape[2]
    page = k_refs[0].shape[1]

    @pl.when(g == 0)
    def _():
        m_ref[...] = jnp.full(m_ref.shape, NEG, F32)
        l_ref[...] = jnp.zeros(l_ref.shape, F32)
        acc_ref[...] = jnp.zeros(acc_ref.shape, F32)

    qd = qd_ref[0]
    logits = []
    for i in range(pages):
        kp = k_refs[i][0].astype(BF16)
        s = lax.dot_general(qd, kp, (((1,), (1,)), ((), ())), preferred_element_type=F32)
        logits.append(s + bp_ref[0, :, i * page:(i + 1) * page])
    s_all = jnp.concatenate(logits, axis=1)
    m_prev = m_ref[...]
    m_new = jnp.maximum(m_prev, jnp.max(s_all, axis=1, keepdims=True))
    alpha = jnp.exp(m_prev - m_new)
    p = jnp.exp(s_all - m_new)
    l_ref[...] = alpha * l_ref[...] + jnp.sum(p, axis=1, keepdims=True)
    m_ref[...] = m_new
    acc = alpha * acc_ref[...]
    for i in range(pages):
        vp = v_refs[i][0].astype(BF16)
        acc = acc + jnp.dot(p[:, i * page:(i + 1) * page].astype(BF16), vp, preferred_element_type=F32)
    acc_ref[...] = acc

    @pl.when(g == pl.num_programs(1) - 1)
    def _():
        s_n = jnp.sum(qd.astype(F32) * kn_ref[0], axis=1, keepdims=True) + bn_ref[0][:, 0:1]
        m_p = m_ref[...]
        m_f = jnp.maximum(m_p, s_n)
        a_f = jnp.exp(m_p - m_f)
        p_n = jnp.exp(s_n - m_f)
        l_f = a_f * l_ref[...] + p_n
        full = (a_f * acc_ref[...] + p_n * vn_ref[0]) / l_f
        head_of_lane = lax.broadcasted_iota(I32, (n_heads, dh), 1) // HEAD_DIM
        head_of_row = lax.broadcasted_iota(I32, (n_heads, dh), 0)
        o_ref[0] = jnp.sum(jnp.where(head_of_lane == head_of_row, full, 0.0), axis=0, keepdims=True)


def _sample_attention(page_table, qd, bias_past, bias_new, k_new, v_new, cache_k, cache_v, pages):
    db, n_pages = page_table.shape
    _, page, dh = cache_k.shape
    n_heads = qd.shape[1]
    page_spec = lambda i: pl.BlockSpec((1, page, dh), lambda b, g, pt: (pt[b, g * pages + i], 0, 0))
    per_b = lambda r, w: pl.BlockSpec((1, r, w), lambda b, g, pt: (b, 0, 0))
    grid_spec = pltpu.PrefetchScalarGridSpec(
        num_scalar_prefetch=1,
        grid=(db, n_pages // pages),
        in_specs=[per_b(n_heads, dh),
                  pl.BlockSpec((1, 1, pages * page), lambda b, g, pt: (b, 0, g)),
                  per_b(1, LANES), per_b(1, dh), per_b(1, dh)]
                 + [page_spec(i) for i in range(pages)] + [page_spec(i) for i in range(pages)],
        out_specs=per_b(1, dh),
        scratch_shapes=[pltpu.VMEM((n_heads, 1), F32), pltpu.VMEM((n_heads, 1), F32),
                        pltpu.VMEM((n_heads, dh), F32)],
    )
    return pl.pallas_call(
        functools.partial(_sample_attn_kernel, pages=pages),
        grid_spec=grid_spec,
        out_shape=jax.ShapeDtypeStruct((db, 1, dh), F32),
        compiler_params=_params(("arbitrary", "arbitrary")),
        name="sample_attention",
    )(page_table, qd, bias_past, bias_new, k_new, v_new, *([cache_k] * pages), *([cache_v] * pages))


def _sample_mix_kernel(x_ref, att_ref, u_ref, vn_ref, g1_ref, sc2_ref, sh2_ref, g2_ref,
                       gw0_ref, gb0_ref, wout_ref, ln1g_ref, ln1b_ref,
                       wup_ref, cw_ref, cb_ref, wdown_ref, ln2g_ref, ln2b_ref, s0_ref, s1_ref,
                       y_ref, gnew_ref, *, alpha, ffc):
    d_att = att_ref.shape[1]
    d_ff = cw_ref.shape[1]
    gm = u_ref[...] * (vn_ref[...] * gw0_ref[...] + gb0_ref[...])
    mix = (jnp.dot(att_ref[...].astype(BF16), wout_ref[0:d_att, :], preferred_element_type=F32)
           + jnp.dot(gm.astype(BF16), wout_ref[d_att:, :], preferred_element_type=F32))
    x1 = _layer_norm(alpha * x_ref[...] + g1_ref[...] * mix, ln1g_ref[...], ln1b_ref[...])
    h2b = (x1 * (1.0 + sc2_ref[...]) + sh2_ref[...]).astype(BF16)
    f = jnp.zeros(x_ref.shape, F32)
    for c in range(d_ff // ffc):
        sl = slice(c * ffc, (c + 1) * ffc)
        g = jnp.dot(h2b, wup_ref[:, sl], preferred_element_type=F32)
        a = jnp.dot(h2b, wup_ref[:, d_ff + c * ffc:d_ff + (c + 1) * ffc], preferred_element_type=F32)
        gnew_ref[:, sl] = g
        g_conv = (cb_ref[:, sl] + cw_ref[0:1, sl] * s0_ref[:, sl] + cw_ref[1:2, sl] * s1_ref[:, sl]
                  + cw_ref[2:3, sl] * g)
        act = g_conv * jax.nn.sigmoid(g_conv) * a
        f = f + jnp.dot(act.astype(BF16), wdown_ref[sl, :], preferred_element_type=F32)
    y_ref[...] = _layer_norm(alpha * x1 + g2_ref[...] * f, ln2g_ref[...], ln2b_ref[...])


def _sample_mix_ffn(x, att, u, vn, g1, sc2, sh2, g2, gw0, gb0, w_out, ln1g, ln1b,
                    w_up, conv_w, conv_b, w_down, ln2g, ln2b, s0, s1, alpha):
    db, d = x.shape
    d_ff = conv_w.shape[1]
    n_in = 21
    return pl.pallas_call(
        functools.partial(_sample_mix_kernel, alpha=alpha, ffc=256),
        in_specs=[_vmem_full()] * n_in,
        out_specs=[_vmem_full(), _vmem_full()],
        out_shape=(jax.ShapeDtypeStruct((db, d), F32), jax.ShapeDtypeStruct((db, d_ff), F32)),
        compiler_params=pltpu.CompilerParams(vmem_limit_bytes=VMEM_LIMIT),
        name="sample_mix_ffn",
    )(x, att, u, vn, g1, sc2, sh2, g2, gw0, gb0, w_out, ln1g, ln1b,
      w_up, conv_w, conv_b, w_down, ln2g, ln2b, s0, s1)


def _rope_table(pos):
    rot = HEAD_DIM // ROPE_FRAC
    half = rot // 2
    inv = ROPE_THETA ** (-jnp.arange(half, dtype=F32) / half)
    ang = pos.astype(F32)[:, None] * inv[None, :]
    cos, sin = jnp.cos(ang), jnp.sin(ang)
    n = pos.shape[0]
    rest = HEAD_DIM - rot
    c64 = jnp.concatenate([cos, cos, jnp.ones((n, rest), F32)], axis=1)
    a64 = jnp.concatenate([-sin, jnp.zeros((n, rest + half), F32)], axis=1)
    b64 = jnp.concatenate([jnp.zeros((n, half), F32), sin, jnp.zeros((n, rest), F32)], axis=1)
    return jnp.concatenate([c64, c64, a64, a64, b64, b64], axis=1)


def _repack_w_in(w_in):
    d = w_in.shape[0]
    o_ki = 4 * 512
    o_u = o_ki + IDX_DIM + IDX_HEADS
    pad = jnp.zeros((d, LANES - IDX_DIM - IDX_HEADS), w_in.dtype)
    return jnp.concatenate([w_in[:, :o_ki], w_in[:, o_ki:o_u], pad, w_in[:, o_u:]], axis=1).astype(BF16)


def _layer(xp, xs, cp, cs, cache_k, cache_v, cache_idx_k, state_conv, page_table,
           w_ada, b_ada, w_in, gm_ln_g, gm_ln_b, gm_ws, gm_bs, w_out, ln1_g, ln1_b,
           w_up, conv_w, conv_b, w_down, ln2_g, ln2_b, alpha):
    b, t, d = xp.shape
    db, ts, _ = xs.shape
    assert b == 1 and ts == 1, "kernels are specialised to one prompt sequence and one decode position"
    n_phys, page = cache_k.shape[0], cache_k.shape[1]
    n_pages = page_table.shape[1]
    past = n_pages * page
    dh = cache_k.shape[2] * cache_k.shape[3]
    n_heads = cache_k.shape[2]
    d_ff = conv_w.shape[1]
    n_groups = gm_ws.shape[0]
    row2 = lambda a: a.reshape(1, -1)

    c_all = jnp.concatenate([cp, cs], axis=0)
    pad_rows = (-c_all.shape[0]) % 8
    c_all = jnp.pad(c_all, ((0, pad_rows), (0, 0)))
    mod = _modulation(c_all, w_ada.astype(BF16), row2(b_ada))
    mp = [mod[0:1, i * d:(i + 1) * d] for i in range(6)]
    ms = [mod[1:1 + db, i * d:(i + 1) * d] for i in range(6)]

    w_in_r = _repack_w_in(w_in)
    w_out_b, w_up_b, w_down_b = w_out.astype(BF16), w_up.astype(BF16), w_down.astype(BF16)
    ln_g, ln_b = row2(gm_ln_g), row2(gm_ln_b)

    x2 = xp.reshape(t, d)
    q_bf, k_f, v_f, k_bf, v_bf, qi_bf, kw, u, vn = _project(
        x2, mp[1], mp[0], w_in_r, _rope_table(jnp.arange(t)), ln_g, ln_b, tm=256)
    n_sel = min(TOPK_MAX, t // 4)
    att = _prompt_attention(q_bf, qi_bf, kw, k_bf, v_bf, n_sel)

    w_causal = (gm_ws * jnp.tril(jnp.ones((CHUNK, CHUNK), gm_ws.dtype))).astype(BF16)
    wpair = jnp.concatenate([w_causal[0::2], w_causal[1::2]], axis=2)
    gbias = jnp.repeat(gm_bs.T, GM_GROUP_DIM, axis=1)
    y_p, conv_p = _prompt_mix_ffn(x2, att, u, vn, mp[2], mp[4], mp[3], mp[5], wpair, gbias, w_out_b,
                                  row2(ln1_g), row2(ln1_b), w_up_b, conv_w, row2(conv_b), w_down_b,
                                  row2(ln2_g), row2(ln2_b), alpha)

    xs2 = xs.reshape(db, d)
    pos_s = jnp.full((db,), past, I32)
    qs_bf, ks_f, vs_f, _, _, qis_bf, kws, us, vns = _project(
        xs2, ms[1], ms[0], w_in_r, _rope_table(pos_s), ln_g, ln_b, tm=db)
    pages = 16
    q8 = qis_bf.reshape(db, IDX_HEADS, IDX_DIM)
    w8 = kws[:, IDX_DIM:IDX_DIM + IDX_HEADS].reshape(db, IDX_HEADS, 1)
    scores = _sample_scores(page_table, q8, w8, cache_idx_k, pages).reshape(db, past)
    n_sel_s = min(TOPK_MAX, (past + ts) // 4)
    bias_past, bias_new = _sample_select(scores, qis_bf.astype(F32), kws, n_sel_s)
    head_mask = (jnp.arange(dh)[None, :] // HEAD_DIM == jnp.arange(n_heads)[:, None])
    qd = jnp.where(head_mask[None], qs_bf[:, None, :], jnp.zeros((), BF16))
    att_s = _sample_attention(page_table, qd, bias_past.reshape(db, 1, past), bias_new.reshape(db, 1, LANES),
                              ks_f.reshape(db, 1, dh), vs_f.reshape(db, 1, dh),
                              cache_k.reshape(n_phys, page, dh), cache_v.reshape(n_phys, page, dh), pages)
    gw0 = row2(jnp.repeat(gm_ws[:, 0, 0], GM_GROUP_DIM))
    gb0 = row2(jnp.repeat(gm_bs[:, 0], GM_GROUP_DIM))
    y_s, g_new = _sample_mix_ffn(xs2, att_s.reshape(db, dh), us, vns, ms[2], ms[4], ms[3], ms[5], gw0, gb0,
                                 w_out_b, row2(ln1_g), row2(ln1_b), w_up_b, conv_w, row2(conv_b), w_down_b,
                                 row2(ln2_g), row2(ln2_b), state_conv[:, 0], state_conv[:, 1], alpha)
    conv_s = jnp.stack([state_conv[:, 1], g_new], axis=1)

    return (y_p.reshape(b, t, d), y_s.reshape(db, ts, d),
            k_f.reshape(b, t, n_heads, HEAD_DIM), v_f.reshape(b, t, n_heads, HEAD_DIM),
            kw[:, :IDX_DIM].reshape(b, t, IDX_DIM), conv_p.reshape(b, CONV_W - 1, d_ff),
            ks_f.reshape(db, ts, n_heads, HEAD_DIM), vs_f.reshape(db, ts, n_heads, HEAD_DIM),
            kws[:, :IDX_DIM].reshape(db, ts, IDX_DIM), vns.reshape(db, ts, -1), conv_s)


def kernel(x_prompt, x_sample, c_prompt, c_sample, cache_k, cache_v, cache_idx_k, state_conv, page_table,
           w_ada, b_ada, w_in, gm_ln_g, gm_ln_b, gm_ws, gm_bs, w_out, ln1_g, ln1_b,
           w_up, conv_w, conv_b, w_down, ln2_g, ln2_b):
    depth = w_ada.shape[0]
    alpha = (2.0 * depth) ** 0.25
    yp, ys = x_prompt, x_sample
    outs = [[] for _ in range(9)]
    for l in range(depth):
        res = _layer(yp, ys, c_prompt, c_sample, cache_k[l], cache_v[l], cache_idx_k[l], state_conv[l], page_table,
                     w_ada[l], b_ada[l], w_in[l], gm_ln_g[l], gm_ln_b[l], gm_ws[l], gm_bs[l], w_out[l],
                     ln1_g[l], ln1_b[l], w_up[l], conv_w[l], conv_b[l], w_down[l], ln2_g[l], ln2_b[l], alpha)
        yp, ys = res[0], res[1]
        for acc, r in zip(outs, res[2:]):
            acc.append(r)
    return (yp, ys) + tuple(jnp.stack(o) for o in outs)
```

```python
import functools

import jax
import jax.numpy as jnp
from jax import lax
from jax.experimental import pallas as pl
from jax.experimental.pallas import tpu as pltpu

F32 = jnp.float32
BF16 = jnp.bfloat16
I32 = jnp.int32

HEAD_DIM = 64
IDX_HEADS = 8
IDX_DIM = 64
TOPK_MAX = 256
ROPE_THETA = 500000.0
ROPE_FRAC = 4
GM_GROUP_DIM = 64
CHUNK = 128
CONV_W = 3
LN_EPS = 1e-5

LANES = 128
INT_MIN = -(2 ** 31)
NEG = -1e30
VMEM_LIMIT = 56 * 1024 * 1024

C_Q, C_K, C_V, C_QI, C_KW, C_U, C_GV, C_END = 0, 512, 1024, 1536, 2048, 2176, 2688, 3200


def _vmem_full():
    return pl.BlockSpec(memory_space=pltpu.VMEM)


def _params(sem):
    return pltpu.CompilerParams(dimension_semantics=sem, vmem_limit_bytes=VMEM_LIMIT)


def _layer_norm(x, g, b):
    mu = jnp.mean(x, axis=-1, keepdims=True)
    xc = x - mu
    var = jnp.mean(xc * xc, axis=-1, keepdims=True)
    return xc * lax.rsqrt(var + LN_EPS) * g + b


def _low_half(rows):
    return lax.broadcasted_iota(I32, (rows, LANES), 1) < HEAD_DIM


def _mod_kernel(c_ref, w_ref, b_ref, o_ref):
    c = c_ref[...]
    s = c * jax.nn.sigmoid(c)
    o_ref[...] = jnp.dot(s.astype(BF16), w_ref[...], preferred_element_type=F32) + b_ref[...]


def _modulation(c, w_ada, b_ada):
    rows, d = c.shape
    n = w_ada.shape[1]
    tn = n // 4
    return pl.pallas_call(
        _mod_kernel,
        grid=(4,),
        in_specs=[pl.BlockSpec((rows, d), lambda j: (0, 0)),
                  pl.BlockSpec((d, tn), lambda j: (0, j)),
                  pl.BlockSpec((1, tn), lambda j: (0, j))],
        out_specs=pl.BlockSpec((rows, tn), lambda j: (0, j)),
        out_shape=jax.ShapeDtypeStruct((rows, n), F32),
        compiler_params=_params(("arbitrary",)),
        name="modulation",
    )(c, w_ada, b_ada)


def _proj_kernel(x_ref, sc_ref, sh_ref, w_ref, rope_ref, lng_ref, lnb_ref,
                 q_ref, kf_ref, vf_ref, kb_ref, vb_ref, qi_ref, kw_ref, u_ref, vn_ref):
    tm = x_ref.shape[0]
    h = x_ref[...] * (1.0 + sc_ref[...]) + sh_ref[...]
    hb = h.astype(BF16)
    cos = rope_ref[:, 0:LANES]
    sin_a = rope_ref[:, LANES:2 * LANES]
    sin_b = rope_ref[:, 2 * LANES:3 * LANES]

    def rope(zc):
        return zc * cos + pltpu.roll(zc, LANES - 8, 1) * sin_a + pltpu.roll(zc, 8, 1) * sin_b

    def mm(c0, c1):
        return jnp.dot(hb, w_ref[:, c0:c1], preferred_element_type=F32)

    zq = mm(C_Q, C_K)
    zk = mm(C_K, C_V)
    zv = mm(C_V, C_QI)
    zqi = mm(C_QI, C_KW)
    for j in range(4):
        sl = slice(j * LANES, (j + 1) * LANES)
        q_ref[:, sl] = (rope(zq[:, sl]) * (HEAD_DIM ** -0.5)).astype(BF16)
        kr = rope(zk[:, sl])
        kf_ref[:, sl] = kr
        kb_ref[:, sl] = kr.astype(BF16)
        qi_ref[:, sl] = (rope(zqi[:, sl]) * (IDX_DIM ** -0.5)).astype(BF16)
    vf_ref[...] = zv
    vb_ref[...] = zv.astype(BF16)
    zkw = mm(C_KW, C_U)
    kw_ref[...] = jnp.where(_low_half(tm), rope(zkw), zkw * (IDX_HEADS ** -0.5))
    u_ref[...] = jax.nn.gelu(mm(C_U, C_GV))
    gv = jax.nn.gelu(mm(C_GV, C_END))
    vn_ref[...] = _layer_norm(gv, lng_ref[...], lnb_ref[...])


def _project(x, sc, sh, w_in_r, rope_tab, ln_g, ln_b, tm):
    rows, d = x.shape
    mod_rows = sc.shape[0]
    tmm = tm if mod_rows == rows else 1
    mod_map = (lambda i: (i, 0)) if mod_rows == rows else (lambda i: (0, 0))
    row = lambda w: pl.BlockSpec((tm, w), lambda i: (i, 0))
    const = lambda a: pl.BlockSpec(a.shape, lambda i: (0, 0))
    dh = 512
    out_shape = (jax.ShapeDtypeStruct((rows, dh), BF16),
                 jax.ShapeDtypeStruct((rows, dh), F32),
                 jax.ShapeDtypeStruct((rows, dh), F32),
                 jax.ShapeDtypeStruct((rows, dh), BF16),
                 jax.ShapeDtypeStruct((rows, dh), BF16),
                 jax.ShapeDtypeStruct((rows, dh), BF16),
                 jax.ShapeDtypeStruct((rows, LANES), F32),
                 jax.ShapeDtypeStruct((rows, dh), F32),
                 jax.ShapeDtypeStruct((rows, dh), F32))
    return pl.pallas_call(
        _proj_kernel,
        grid=(rows // tm,),
        in_specs=[row(d),
                  pl.BlockSpec((tmm, d), mod_map), pl.BlockSpec((tmm, d), mod_map),
                  _vmem_full(), row(3 * LANES), const(ln_g), const(ln_b)],
        out_specs=[row(dh), row(dh), row(dh), row(dh), row(dh), row(dh), row(LANES), row(dh), row(dh)],
        out_shape=out_shape,
        compiler_params=_params(("arbitrary",)),
        name="projection",
    )(x, sc, sh, w_in_r, rope_tab, ln_g, ln_b)


def _ordered_key(score):
    bits = lax.bitcast_convert_type(score, I32)
    return bits ^ ((bits >> 31) & 0x7FFFFFFF)


COUNT_UNROLL = 4


def _count(keys_ref, n_chunks, rows, pred):
    def body(g, acc):
        for u in range(COUNT_UNROLL):
            c = g * COUNT_UNROLL + u
            acc = acc + jnp.where(pred(keys_ref[c], c), 1.0, 0.0)
        return acc
    acc = lax.fori_loop(0, n_chunks // COUNT_UNROLL, body, jnp.zeros((rows, LANES), F32))
    return jnp.sum(acc, axis=1, keepdims=True)


def _topk_threshold(keys_ref, n_chunks, rows, n_sel, idx_bits):
    def unsettled(cnt_t):
        return jnp.max(jnp.where(cnt_t == n_sel, 0.0, 1.0)) > 0.0

    def value_cond(carry):
        i, _, _, go = carry
        return jnp.logical_and(i < 32, go)

    def value_pass(carry):
        i, t, cnt_t, _ = carry
        cand = t + jnp.left_shift(jnp.int32(1), 31 - i)
        cand_b = jnp.broadcast_to(cand, (rows, LANES))
        cnt = _count(keys_ref, n_chunks, rows, lambda k, c: k >= cand_b)
        keep = cnt >= n_sel
        cnt_t = jnp.where(keep, cnt, cnt_t)
        return i + 1, jnp.where(keep, cand, t), cnt_t, unsettled(cnt_t)

    cnt0 = jnp.full((rows, 1), 2.0 ** 30, F32)
    _, t, cnt_ge, _ = lax.while_loop(
        value_cond, value_pass, (jnp.int32(0), jnp.full((rows, 1), INT_MIN, I32), cnt0, unsettled(cnt0)))
    t_b = jnp.broadcast_to(t, (rows, LANES))
    tied = jnp.logical_and(cnt_ge > n_sel, t > INT_MIN)
    any_tied = jnp.max(jnp.where(tied, 1.0, 0.0)) > 0.0

    @pl.when(any_tied)
    def _():
        cnt_gt = _count(keys_ref, n_chunks, rows, lambda k, c: k > t_b)
        need = n_sel - cnt_gt
        lane = lax.broadcasted_iota(I32, (rows, LANES), 1)

        def index_pass(i, j):
            cand = j + jnp.left_shift(jnp.int32(1), idx_bits - 1 - i)
            cand_b = jnp.broadcast_to(cand, (rows, LANES))
            cnt = _count(keys_ref, n_chunks, rows,
                         lambda k, c: jnp.where(k == t_b, c * LANES + lane, cand_b) < cand_b)
            return jnp.where(cnt < need, cand, j)

        j = lax.fori_loop(0, idx_bits, index_pass, jnp.zeros((rows, 1), I32))
        j_b = jnp.broadcast_to(jnp.where(tied, j, jnp.int32(2 ** 31 - 1)), (rows, LANES))

        def drop(c, carry):
            k = keys_ref[c]
            over = jnp.where(k == t_b, c * LANES + lane, j_b) > j_b
            keys_ref[c] = jnp.where(over, k - 1, k)
            return carry
        lax.fori_loop(0, n_chunks, drop, 0)

    return jnp.maximum(t, INT_MIN + 1)


def _prompt_attn_kernel(q_ref, qi_ref, kw_ref, kt_ref, v_ref, kit_ref, o_ref,
                        keys_ref, bias_ref, qm_ref, qim_ref, wb_ref, m_ref, l_ref, alpha_ref, acc_ref,
                        s_ref, p_ref, *, tq, tkb, n_sel, idx_bits):
    qb = pl.program_id(0)
    q0 = qb * tq
    n_kb = (q0 + tq - 1) // tkb + 1
    cpb = tkb // LANES
    low = _low_half(tq)
    lane = lax.broadcasted_iota(I32, (tq, LANES), 1)
    zero_b = jnp.zeros((tq, LANES), BF16)

    for j in range(4):
        sl = slice(j * LANES, (j + 1) * LANES)
        qc = q_ref[:, sl]
        qm_ref[2 * j] = jnp.where(low, qc, zero_b)
        qm_ref[2 * j + 1] = jnp.where(low, zero_b, qc)
        qic = qi_ref[:, sl]
        qim_ref[2 * j] = jnp.where(low, qic, zero_b)
        qim_ref[2 * j + 1] = jnp.where(low, zero_b, qic)
    kw = kw_ref[...]
    for h in range(IDX_HEADS):
        w_h = jnp.sum(jnp.where(lane == IDX_DIM + h, kw, 0.0), axis=1, keepdims=True)
        wb_ref[h] = jnp.broadcast_to(w_h, (tq, LANES))

    q_pos = q0 + lax.broadcasted_iota(I32, (tq, LANES), 0)

    def score_block(kb, carry):
        kit = kit_ref[kb]
        acc = [jnp.zeros((tq, LANES), F32) for _ in range(cpb)]
        for h in range(IDX_HEADS):
            s = jnp.dot(qim_ref[h], kit, preferred_element_type=F32)
            w_h = wb_ref[h]
            for c in range(cpb):
                acc[c] = acc[c] + jnp.maximum(s[:, c * LANES:(c + 1) * LANES], 0.0) * w_h
        for c in range(cpb):
            key_pos = kb * tkb + c * LANES + lane
            keys_ref[kb * cpb + c] = jnp.where(key_pos <= q_pos, _ordered_key(acc[c]), INT_MIN)
        return carry

    lax.fori_loop(0, n_kb, score_block, 0)

    thr = _topk_threshold(keys_ref, n_kb * cpb, tq, n_sel, idx_bits)
    thr_b = jnp.broadcast_to(thr, (tq, LANES))

    m_ref[...] = jnp.full(m_ref.shape, NEG, F32)
    l_ref[...] = jnp.zeros(l_ref.shape, F32)
    acc_ref[...] = jnp.zeros(acc_ref.shape, F32)
    n_heads = qm_ref.shape[0]

    def attend_block(kb, carry):
        for c in range(cpb):
            bias_ref[:, c * LANES:(c + 1) * LANES] = jnp.where(keys_ref[kb * cpb + c] >= thr_b, 0.0, NEG)
        for h in range(n_heads):
            kt2 = kt_ref[kb, (h // 2) * LANES:(h // 2 + 1) * LANES, :]
            s = jnp.dot(qm_ref[h], kt2, preferred_element_type=F32) + bias_ref[...]
            s_ref[h] = s
            cm = s[:, 0:LANES]
            for c in range(1, cpb):
                cm = jnp.maximum(cm, s[:, c * LANES:(c + 1) * LANES])
            m_prev = m_ref[h]
            m_new = jnp.maximum(m_prev, jnp.max(cm, axis=1, keepdims=True))
            alpha_ref[h] = jnp.exp(m_prev - m_new)
            m_ref[h] = m_new
        for h in range(n_heads):
            m_new = m_ref[h]
            p_sum = None
            for c in range(cpb):
                sl = slice(c * LANES, (c + 1) * LANES)
                p = jnp.exp(s_ref[h, :, sl] - m_new)
                p_ref[h, :, sl] = p.astype(BF16)
                p_sum = p if p_sum is None else p_sum + p
            l_ref[h] = alpha_ref[h] * l_ref[h] + p_sum
        for j in range(n_heads // 2):
            v2 = v_ref[kb, :, j * LANES:(j + 1) * LANES]
            pv_e = jnp.dot(p_ref[2 * j], v2, preferred_element_type=F32)
            pv_o = jnp.dot(p_ref[2 * j + 1], v2, preferred_element_type=F32)
            a = acc_ref[j]
            acc_ref[j] = jnp.where(low, alpha_ref[2 * j] * a + pv_e, alpha_ref[2 * j + 1] * a + pv_o)
        return carry

    lax.fori_loop(0, n_kb, attend_block, 0)

    for j in range(n_heads // 2):
        l_e = jnp.sum(l_ref[2 * j], axis=1, keepdims=True)
        l_o = jnp.sum(l_ref[2 * j + 1], axis=1, keepdims=True)
        inv = jnp.where(low, 1.0 / l_e, 1.0 / l_o)
        o_ref[:, j * LANES:(j + 1) * LANES] = (acc_ref[j] * inv).astype(o_ref.dtype)


def _prompt_attention(q_bf, qi_bf, kw, k_bf, v_bf, n_sel):
    t, dh = q_bf.shape
    tq, tkb = 128, 512
    nb = t // tkb
    n_heads = dh // HEAD_DIM
    kt3 = k_bf.reshape(nb, tkb, dh).transpose(0, 2, 1)
    v3 = v_bf.reshape(nb, tkb, dh)
    kit = kw[:, :IDX_DIM].astype(BF16).reshape(nb, tkb, IDX_DIM).transpose(0, 2, 1)
    kit2 = jnp.concatenate([kit, kit], axis=1)
    kern = functools.partial(_prompt_attn_kernel, tq=tq, tkb=tkb, n_sel=n_sel,
                             idx_bits=max(1, (t - 1).bit_length()))
    row = lambda w: pl.BlockSpec((tq, w), lambda i: (i, 0))
    return pl.pallas_call(
        kern,
        grid=(t // tq,),
        in_specs=[row(dh), row(dh), row(LANES), _vmem_full(), _vmem_full(), _vmem_full()],
        out_specs=row(dh),
        out_shape=jax.ShapeDtypeStruct((t, dh), BF16),
        scratch_shapes=[pltpu.VMEM((t // LANES, tq, LANES), I32),
                        pltpu.VMEM((tq, tkb), F32),
                        pltpu.VMEM((n_heads, tq, LANES), BF16),
                        pltpu.VMEM((IDX_HEADS, tq, LANES), BF16),
                        pltpu.VMEM((IDX_HEADS, tq, LANES), F32),
                        pltpu.VMEM((n_heads, tq, LANES), F32),
                        pltpu.VMEM((n_heads, tq, LANES), F32),
                        pltpu.VMEM((n_heads, tq, LANES), F32),
                        pltpu.VMEM((n_heads // 2, tq, LANES), F32),
                        pltpu.VMEM((n_heads, tq, tkb), F32),
                        pltpu.VMEM((n_heads, tq, tkb), BF16)],
        compiler_params=_params(("arbitrary",)),
        name="prompt_attention",
    )(q_bf, qi_bf, kw, kt3, v3, kit2)


def _gmlp_gate(u, vn, wp_ref, gb_ref, low):
    cols = []
    for j in range(4):
        sl = slice(j * LANES, (j + 1) * LANES)
        col = vn[:, sl]
        rhs = jnp.concatenate([jnp.where(low, col, 0.0), jnp.where(low, 0.0, col)], axis=0).astype(BF16)
        mixed = jnp.dot(wp_ref[j], rhs, preferred_element_type=F32) + gb_ref[:, sl]
        cols.append(u[:, sl] * mixed)
    return jnp.concatenate(cols, axis=1)


def _prompt_mix_kernel(x_ref, att_ref, u_ref, vn_ref, g1_ref, sc2_ref, sh2_ref, g2_ref,
                       wp_ref, gb_ref, wout_ref, ln1g_ref, ln1b_ref,
                       wup_ref, cw_ref, cb_ref, wdown_ref, ln2g_ref, ln2b_ref,
                       y_ref, conv_ref, gbuf_ref, *, alpha, ffc):
    tm = x_ref.shape[0]
    d_att = att_ref.shape[1]
    d_ff = cw_ref.shape[1]
    low = _low_half(CHUNK)

    @pl.when(pl.program_id(0) == 0)
    def _():
        gbuf_ref[0:8, :] = jnp.zeros((8, d_ff), F32)

    gm = jnp.concatenate(
        [_gmlp_gate(u_ref[c * CHUNK:(c + 1) * CHUNK, :], vn_ref[c * CHUNK:(c + 1) * CHUNK, :], wp_ref, gb_ref, low)
         for c in range(tm // CHUNK)], axis=0)
    mix = (jnp.dot(att_ref[...], wout_ref[0:d_att, :], preferred_element_type=F32)
           + jnp.dot(gm.astype(BF16), wout_ref[d_att:, :], preferred_element_type=F32))
    x1 = _layer_norm(alpha * x_ref[...] + g1_ref[...] * mix, ln1g_ref[...], ln1b_ref[...])
    h2b = (x1 * (1.0 + sc2_ref[...]) + sh2_ref[...]).astype(BF16)

    f = jnp.zeros((tm, x_ref.shape[1]), F32)
    for c in range(d_ff // ffc):
        sl = slice(c * ffc, (c + 1) * ffc)
        g = jnp.dot(h2b, wup_ref[:, sl], preferred_element_type=F32)
        a = jnp.dot(h2b, wup_ref[:, d_ff + c * ffc:d_ff + (c + 1) * ffc], preferred_element_type=F32)
        gbuf_ref[8:8 + tm, sl] = g
        g_conv = (cb_ref[:, sl] + cw_ref[0:1, sl] * gbuf_ref[6:6 + tm, sl]
                  + cw_ref[1:2, sl] * gbuf_ref[7:7 + tm, sl] + cw_ref[2:3, sl] * g)
        act = g_conv * jax.nn.sigmoid(g_conv) * a
        f = f + jnp.dot(act.astype(BF16), wdown_ref[sl, :], preferred_element_type=F32)
        gbuf_ref[6:8, sl] = gbuf_ref[tm + 6:tm + 8, sl]
    y_ref[...] = _layer_norm(alpha * x1 + g2_ref[...] * f, ln2g_ref[...], ln2b_ref[...])
    conv_ref[...] = gbuf_ref[6:8, :]


def _prompt_mix_ffn(x, att, u, vn, g1, sc2, sh2, g2, wpair, gbias, w_out, ln1g, ln1b,
                    w_up, conv_w, conv_b, w_down, ln2g, ln2b, alpha):
    t, d = x.shape
    tm = 256
    d_ff = conv_w.shape[1]
    row = lambda w: pl.BlockSpec((tm, w), lambda i: (i, 0))
    const = lambda a: pl.BlockSpec(a.shape, lambda i: (0,) * a.ndim)
    kern = functools.partial(_prompt_mix_kernel, alpha=alpha, ffc=256)
    return pl.pallas_call(
        kern,
        grid=(t // tm,),
        in_specs=[row(d), row(att.shape[1]), row(u.shape[1]), row(vn.shape[1]),
                  const(g1), const(sc2), const(sh2), const(g2),
                  _vmem_full(), const(gbias), _vmem_full(), const(ln1g), const(ln1b),
                  _vmem_full(), const(conv_w), const(conv_b), _vmem_full(), const(ln2g), const(ln2b)],
        out_specs=[row(d), pl.BlockSpec((CONV_W - 1, d_ff), lambda i: (0, 0))],
        out_shape=(jax.ShapeDtypeStruct((t, d), F32), jax.ShapeDtypeStruct((CONV_W - 1, d_ff), F32)),
        scratch_shapes=[pltpu.VMEM((tm + 8, d_ff), F32)],
        compiler_params=_params(("arbitrary",)),
        name="prompt_mix_ffn",
    )(x, att, u, vn, g1, sc2, sh2, g2, wpair, gbias, w_out, ln1g, ln1b,
      w_up, conv_w, conv_b, w_down, ln2g, ln2b)


def _sample_score_kernel(pt_ref, q8_ref, w8_ref, *refs, pages):
    page_refs, o_ref = refs[:pages], refs[pages]
    q8 = q8_ref[0]
    w8 = w8_ref[0]
    for i in range(pages):
        kit = page_refs[i][0].astype(BF16)
        s = jnp.dot(q8, kit, preferred_element_type=F32)
        sc = jnp.sum(jnp.maximum(s, 0.0) * w8, axis=0, keepdims=True)
        o_ref[0, :, i * kit.shape[1]:(i + 1) * kit.shape[1]] = sc


def _sample_scores(page_table, q8, w8, cache_idx_kt, pages):
    db, n_pages = page_table.shape
    _, di, page = cache_idx_kt.shape
    page_spec = lambda i: pl.BlockSpec((1, di, page), lambda b, g, pt: (pt[b, g * pages + i], 0, 0))
    grid_spec = pltpu.PrefetchScalarGridSpec(
        num_scalar_prefetch=1,
        grid=(db, n_pages // pages),
        in_specs=[pl.BlockSpec((1, IDX_HEADS, di), lambda b, g, pt: (b, 0, 0)),
                  pl.BlockSpec((1, IDX_HEADS, 1), lambda b, g, pt: (b, 0, 0))]
                 + [page_spec(i) for i in range(pages)],
        out_specs=pl.BlockSpec((1, 1, pages * page), lambda b, g, pt: (b, 0, g)),
    )
    return pl.pallas_call(
        functools.partial(_sample_score_kernel, pages=pages),
        grid_spec=grid_spec,
        out_shape=jax.ShapeDtypeStruct((db, 1, n_pages * page), F32),
        compiler_params=_params(("arbitrary", "arbitrary")),
        name="sample_scores",
    )(page_table, q8, w8, *([cache_idx_kt] * pages))


def _sample_select_kernel(sc_ref, qi_ref, kw_ref, bp_ref, bn_ref, keys_ref, *, n_sel, idx_bits):
    db, past = sc_ref.shape
    n_past = past // LANES
    low = _low_half(db)
    lane = lax.broadcasted_iota(I32, (db, LANES), 1)
    kw = kw_ref[...]
    ki2 = jnp.where(low, kw, pltpu.roll(kw, HEAD_DIM, 1))
    s_new = jnp.zeros((db, 1), F32)
    for j in range(4):
        prod = qi_ref[:, j * LANES:(j + 1) * LANES] * ki2
        for e in range(2):
            h = 2 * j + e
            s_h = jnp.sum(jnp.where(low, prod, 0.0) if e == 0 else jnp.where(low, 0.0, prod), axis=1, keepdims=True)
            w_h = jnp.sum(jnp.where(lane == IDX_DIM + h, kw, 0.0), axis=1, keepdims=True)
            s_new = s_new + jnp.maximum(s_h, 0.0) * w_h
    for c in range(n_past):
        keys_ref[c] = _ordered_key(sc_ref[:, c * LANES:(c + 1) * LANES])
    keys_ref[n_past] = jnp.where(lane == 0, _ordered_key(jnp.broadcast_to(s_new, (db, LANES))), INT_MIN)
    for c in range(n_past + 1, keys_ref.shape[0]):
        keys_ref[c] = jnp.full((db, LANES), INT_MIN, I32)
    thr = _topk_threshold(keys_ref, keys_ref.shape[0], db, n_sel, idx_bits)
    thr_b = jnp.broadcast_to(thr, (db, LANES))
    for c in range(n_past):
        bp_ref[:, c * LANES:(c + 1) * LANES] = jnp.where(keys_ref[c] >= thr_b, 0.0, NEG)
    bn_ref[...] = jnp.where(keys_ref[n_past] >= thr_b, 0.0, NEG)


def _sample_select(scores, qi_f, kw, n_sel):
    db, past = scores.shape
    kern = functools.partial(_sample_select_kernel, n_sel=n_sel, idx_bits=past.bit_length())
    return pl.pallas_call(
        kern,
        in_specs=[_vmem_full(), _vmem_full(), _vmem_full()],
        out_specs=[_vmem_full(), _vmem_full()],
        out_shape=(jax.ShapeDtypeStruct((db, past), F32), jax.ShapeDtypeStruct((db, LANES), F32)),
        scratch_shapes=[pltpu.VMEM((-(-(past // LANES + 1) // COUNT_UNROLL) * COUNT_UNROLL, db, LANES), I32)],
        compiler_params=pltpu.CompilerParams(vmem_limit_bytes=VMEM_LIMIT),
        name="sample_select",
    )(scores, qi_f, kw)


def _sample_attn_kernel(pt_ref, qd_ref, bp_ref, bn_ref, kn_ref, vn_ref, *refs, pages):
    k_refs, v_refs = refs[:pages], refs[pages:2 * pages]
    o_ref, m_ref, l_ref, acc_ref = refs[2 * pages:]
    g = pl.program_id(1)
    n_heads, dh = qd_ref.shape[1], qd_ref.shape[2]
    page = k_refs[0].shape[2]

    @pl.when(g == 0)
    def _():
        m_ref[...] = jnp.full(m_ref.shape, NEG, F32)
        l_ref[...] = jnp.zeros(l_ref.shape, F32)
        acc_ref[...] = jnp.zeros(acc_ref.shape, F32)

    qd = qd_ref[0]
    logits = []
    for i in range(pages):
        ktp = k_refs[i][0].astype(BF16)
        s = jnp.dot(qd, ktp, preferred_element_type=F32)
        logits.append(s + bp_ref[0, :, i * page:(i + 1) * page])
    s_all = jnp.concatenate(logits, axis=1)
    m_prev = m_ref[...]
    m_new = jnp.maximum(m_prev, jnp.max(s_all, axis=1, keepdims=True))
    alpha = jnp.exp(m_prev - m_new)
    p = jnp.exp(s_all - m_new)
    l_ref[...] = alpha * l_ref[...] + jnp.sum(p, axis=1, keepdims=True)
    m_ref[...] = m_new
    acc = alpha * acc_ref[...]
    for i in range(pages):
        vtp = v_refs[i][0].astype(BF16)
        acc = acc + lax.dot_general(p[:, i * page:(i + 1) * page].astype(BF16), vtp,
                                    (((1,), (1,)), ((), ())), preferred_element_type=F32)
    acc_ref[...] = acc

    @pl.when(g == pl.num_programs(1) - 1)
    def _():
        s_n = jnp.sum(qd.astype(F32) * kn_ref[0], axis=1, keepdims=True) + bn_ref[0][:, 0:1]
        m_p = m_ref[...]
        m_f = jnp.maximum(m_p, s_n)
        a_f = jnp.exp(m_p - m_f)
        p_n = jnp.exp(s_n - m_f)
        l_f = a_f * l_ref[...] + p_n
        full = (a_f * acc_ref[...] + p_n * vn_ref[0]) / l_f
        head_of_lane = lax.broadcasted_iota(I32, (n_heads, dh), 1) // HEAD_DIM
        head_of_row = lax.broadcasted_iota(I32, (n_heads, dh), 0)
        o_ref[0] = jnp.sum(jnp.where(head_of_lane == head_of_row, full, 0.0), axis=0, keepdims=True)


def _sample_attention(page_table, qd, bias_past, bias_new, k_new, v_new, cache_kt, cache_vt, pages):
    db, n_pages = page_table.shape
    _, dh, page = cache_kt.shape
    n_heads = qd.shape[1]
    page_spec = lambda i: pl.BlockSpec((1, dh, page), lambda b, g, pt: (pt[b, g * pages + i], 0, 0))
    per_b = lambda r, w: pl.BlockSpec((1, r, w), lambda b, g, pt: (b, 0, 0))
    grid_spec = pltpu.PrefetchScalarGridSpec(
        num_scalar_prefetch=1,
        grid=(db, n_pages // pages),
        in_specs=[per_b(n_heads, dh),
                  pl.BlockSpec((1, 1, pages * page), lambda b, g, pt: (b, 0, g)),
                  per_b(1, LANES), per_b(1, dh), per_b(1, dh)]
                 + [page_spec(i) for i in range(pages)] + [page_spec(i) for i in range(pages)],
        out_specs=per_b(1, dh),
        scratch_shapes=[pltpu.VMEM((n_heads, 1), F32), pltpu.VMEM((n_heads, 1), F32),
                        pltpu.VMEM((n_heads, dh), F32)],
    )
    return pl.pallas_call(
        functools.partial(_sample_attn_kernel, pages=pages),
        grid_spec=grid_spec,
        out_shape=jax.ShapeDtypeStruct((db, 1, dh), F32),
        compiler_params=_params(("arbitrary", "arbitrary")),
        name="sample_attention",
    )(page_table, qd, bias_past, bias_new, k_new, v_new, *([cache_kt] * pages), *([cache_vt] * pages))


def _sample_mix_kernel(x_ref, att_ref, u_ref, vn_ref, g1_ref, sc2_ref, sh2_ref, g2_ref,
                       gw0_ref, gb0_ref, wout_ref, ln1g_ref, ln1b_ref,
                       wup_ref, cw_ref, cb_ref, wdown_ref, ln2g_ref, ln2b_ref, s0_ref, s1_ref,
                       y_ref, gnew_ref, *, alpha, ffc):
    d_att = att_ref.shape[1]
    d_ff = cw_ref.shape[1]
    gm = u_ref[...] * (vn_ref[...] * gw0_ref[...] + gb0_ref[...])
    mix = (jnp.dot(att_ref[...].astype(BF16), wout_ref[0:d_att, :], preferred_element_type=F32)
           + jnp.dot(gm.astype(BF16), wout_ref[d_att:, :], preferred_element_type=F32))
    x1 = _layer_norm(alpha * x_ref[...] + g1_ref[...] * mix, ln1g_ref[...], ln1b_ref[...])
    h2b = (x1 * (1.0 + sc2_ref[...]) + sh2_ref[...]).astype(BF16)
    f = jnp.zeros(x_ref.shape, F32)
    for c in range(d_ff // ffc):
        sl = slice(c * ffc, (c + 1) * ffc)
        g = jnp.dot(h2b, wup_ref[:, sl], preferred_element_type=F32)
        a = jnp.dot(h2b, wup_ref[:, d_ff + c * ffc:d_ff + (c + 1) * ffc], preferred_element_type=F32)
        gnew_ref[:, sl] = g
        g_conv = (cb_ref[:, sl] + cw_ref[0:1, sl] * s0_ref[:, sl] + cw_ref[1:2, sl] * s1_ref[:, sl]
                  + cw_ref[2:3, sl] * g)
        act = g_conv * jax.nn.sigmoid(g_conv) * a
        f = f + jnp.dot(act.astype(BF16), wdown_ref[sl, :], preferred_element_type=F32)
    y_ref[...] = _layer_norm(alpha * x1 + g2_ref[...] * f, ln2g_ref[...], ln2b_ref[...])


def _sample_mix_ffn(x, att, u, vn, g1, sc2, sh2, g2, gw0, gb0, w_out, ln1g, ln1b,
                    w_up, conv_w, conv_b, w_down, ln2g, ln2b, s0, s1, alpha):
    db, d = x.shape
    d_ff = conv_w.shape[1]
    n_in = 21
    return pl.pallas_call(
        functools.partial(_sample_mix_kernel, alpha=alpha, ffc=256),
        in_specs=[_vmem_full()] * n_in,
        out_specs=[_vmem_full(), _vmem_full()],
        out_shape=(jax.ShapeDtypeStruct((db, d), F32), jax.ShapeDtypeStruct((db, d_ff), F32)),
        compiler_params=pltpu.CompilerParams(vmem_limit_bytes=VMEM_LIMIT),
        name="sample_mix_ffn",
    )(x, att, u, vn, g1, sc2, sh2, g2, gw0, gb0, w_out, ln1g, ln1b,
      w_up, conv_w, conv_b, w_down, ln2g, ln2b, s0, s1)


def _rope_table(pos):
    rot = HEAD_DIM // ROPE_FRAC
    half = rot // 2
    inv = ROPE_THETA ** (-jnp.arange(half, dtype=F32) / half)
    ang = pos.astype(F32)[:, None] * inv[None, :]
    cos, sin = jnp.cos(ang), jnp.sin(ang)
    n = pos.shape[0]
    rest = HEAD_DIM - rot
    c64 = jnp.concatenate([cos, cos, jnp.ones((n, rest), F32)], axis=1)
    a64 = jnp.concatenate([-sin, jnp.zeros((n, rest + half), F32)], axis=1)
    b64 = jnp.concatenate([jnp.zeros((n, half), F32), sin, jnp.zeros((n, rest), F32)], axis=1)
    return jnp.concatenate([c64, c64, a64, a64, b64, b64], axis=1)


def _repack_w_in(w_in):
    d = w_in.shape[0]
    o_ki = 4 * 512
    o_u = o_ki + IDX_DIM + IDX_HEADS
    pad = jnp.zeros((d, LANES - IDX_DIM - IDX_HEADS), w_in.dtype)
    return jnp.concatenate([w_in[:, :o_ki], w_in[:, o_ki:o_u], pad, w_in[:, o_u:]], axis=1).astype(BF16)


def _layer(xp, xs, cp, cs, cache_k, cache_v, cache_idx_k, state_conv, page_table,
           w_ada, b_ada, w_in, gm_ln_g, gm_ln_b, gm_ws, gm_bs, w_out, ln1_g, ln1_b,
           w_up, conv_w, conv_b, w_down, ln2_g, ln2_b, alpha):
    b, t, d = xp.shape
    db, ts, _ = xs.shape
    assert b == 1 and ts == 1, "kernels are specialised to one prompt sequence and one decode position"
    n_phys, page = cache_k.shape[0], cache_k.shape[1]
    n_pages = page_table.shape[1]
    past = n_pages * page
    dh = cache_k.shape[2] * cache_k.shape[3]
    n_heads = cache_k.shape[2]
    d_ff = conv_w.shape[1]
    n_groups = gm_ws.shape[0]
    row2 = lambda a: a.reshape(1, -1)

    c_all = jnp.concatenate([cp, cs], axis=0)
    pad_rows = (-c_all.shape[0]) % 8
    c_all = jnp.pad(c_all, ((0, pad_rows), (0, 0)))
    mod = _modulation(c_all, w_ada.astype(BF16), row2(b_ada))
    mp = [mod[0:1, i * d:(i + 1) * d] for i in range(6)]
    ms = [mod[1:1 + db, i * d:(i + 1) * d] for i in range(6)]

    w_in_r = _repack_w_in(w_in)
    w_out_b, w_up_b, w_down_b = w_out.astype(BF16), w_up.astype(BF16), w_down.astype(BF16)
    ln_g, ln_b = row2(gm_ln_g), row2(gm_ln_b)

    x2 = xp.reshape(t, d)
    q_bf, k_f, v_f, k_bf, v_bf, qi_bf, kw, u, vn = _project(
        x2, mp[1], mp[0], w_in_r, _rope_table(jnp.arange(t)), ln_g, ln_b, tm=256)
    n_sel = min(TOPK_MAX, t // 4)
    att = _prompt_attention(q_bf, qi_bf, kw, k_bf, v_bf, n_sel)

    w_causal = (gm_ws * jnp.tril(jnp.ones((CHUNK, CHUNK), gm_ws.dtype))).astype(BF16)
    wpair = jnp.concatenate([w_causal[0::2], w_causal[1::2]], axis=2)
    gbias = jnp.repeat(gm_bs.T, GM_GROUP_DIM, axis=1)
    y_p, conv_p = _prompt_mix_ffn(x2, att, u, vn, mp[2], mp[4], mp[3], mp[5], wpair, gbias, w_out_b,
                                  row2(ln1_g), row2(ln1_b), w_up_b, conv_w, row2(conv_b), w_down_b,
                                  row2(ln2_g), row2(ln2_b), alpha)

    xs2 = xs.reshape(db, d)
    pos_s = jnp.full((db,), past, I32)
    qs_bf, ks_f, vs_f, _, _, qis_bf, kws, us, vns = _project(
        xs2, ms[1], ms[0], w_in_r, _rope_table(pos_s), ln_g, ln_b, tm=db)
    pages = 16
    q8 = qis_bf.reshape(db, IDX_HEADS, IDX_DIM)
    w8 = kws[:, IDX_DIM:IDX_DIM + IDX_HEADS].reshape(db, IDX_HEADS, 1)
    cache_idx_kt = jnp.swapaxes(cache_idx_k, 1, 2)
    cache_kt = jnp.transpose(cache_k, (0, 2, 3, 1)).reshape(n_phys, dh, page)
    cache_vt = jnp.transpose(cache_v, (0, 2, 3, 1)).reshape(n_phys, dh, page)
    scores = _sample_scores(page_table, q8, w8, cache_idx_kt, pages).reshape(db, past)
    n_sel_s = min(TOPK_MAX, (past + ts) // 4)
    bias_past, bias_new = _sample_select(scores, qis_bf.astype(F32), kws, n_sel_s)
    head_mask = (jnp.arange(dh)[None, :] // HEAD_DIM == jnp.arange(n_heads)[:, None])
    qd = jnp.where(head_mask[None], qs_bf[:, None, :], jnp.zeros((), BF16))
    att_s = _sample_attention(page_table, qd, bias_past.reshape(db, 1, past), bias_new.reshape(db, 1, LANES),
                              ks_f.reshape(db, 1, dh), vs_f.reshape(db, 1, dh),
                              cache_kt, cache_vt, pages)
    gw0 = row2(jnp.repeat(gm_ws[:, 0, 0], GM_GROUP_DIM))
    gb0 = row2(jnp.repeat(gm_bs[:, 0], GM_GROUP_DIM))
    y_s, g_new = _sample_mix_ffn(xs2, att_s.reshape(db, dh), us, vns, ms[2], ms[4], ms[3], ms[5], gw0, gb0,
                                 w_out_b, row2(ln1_g), row2(ln1_b), w_up_b, conv_w, row2(conv_b), w_down_b,
                                 row2(ln2_g), row2(ln2_b), state_conv[:, 0], state_conv[:, 1], alpha)
    conv_s = jnp.stack([state_conv[:, 1], g_new], axis=1)

    return (y_p.reshape(b, t, d), y_s.reshape(db, ts, d),
            k_f.reshape(b, t, n_heads, HEAD_DIM), v_f.reshape(b, t, n_heads, HEAD_DIM),
            kw[:, :IDX_DIM].reshape(b, t, IDX_DIM), conv_p.reshape(b, CONV_W - 1, d_ff),
            ks_f.reshape(db, ts, n_heads, HEAD_DIM), vs_f.reshape(db, ts, n_heads, HEAD_DIM),
            kws[:, :IDX_DIM].reshape(db, ts, IDX_DIM), vns.reshape(db, ts, -1), conv_s)


def kernel(x_prompt, x_sample, c_prompt, c_sample, cache_k, cache_v, cache_idx_k, state_conv, page_table,
           w_ada, b_ada, w_in, gm_ln_g, gm_ln_b, gm_ws, gm_bs, w_out, ln1_g, ln1_b,
           w_up, conv_w, conv_b, w_down, ln2_g, ln2_b):
    depth = w_ada.shape[0]
    alpha = (2.0 * depth) ** 0.25
    yp, ys = x_prompt, x_sample
    outs = [[] for _ in range(9)]
    for l in range(depth):
        res = _layer(yp, ys, c_prompt, c_sample, cache_k[l], cache_v[l], cache_idx_k[l], state_conv[l], page_table,
                     w_ada[l], b_ada[l], w_in[l], gm_ln_g[l], gm_ln_b[l], gm_ws[l], gm_bs[l], w_out[l],
                     ln1_g[l], ln1_b[l], w_up[l], conv_w[l], conv_b[l], w_down[l], ln2_g[l], ln2_b[l], alpha)
        yp, ys = res[0], res[1]
        for acc, r in zip(outs, res[2:]):
            acc.append(r)
    return (yp, ys) + tuple(jnp.stack(o) for o in outs)
```

```python
import functools

import jax
import jax.numpy as jnp
from jax import lax
from jax.experimental import pallas as pl
from jax.experimental.pallas import tpu as pltpu

F32 = jnp.float32
BF16 = jnp.bfloat16
I32 = jnp.int32

HEAD_DIM = 64
IDX_HEADS = 8
IDX_DIM = 64
TOPK_MAX = 256
ROPE_THETA = 500000.0
ROPE_FRAC = 4
GM_GROUP_DIM = 64
CHUNK = 128
CONV_W = 3
LN_EPS = 1e-5

Q_SCALE = HEAD_DIM ** -0.5 * 1.4426950408889634

LANES = 128
INT_MIN = -(2 ** 31)
NEG = -1e30
VMEM_LIMIT = 61 * 1024 * 1024

C_Q, C_K, C_V, C_QI, C_KW, C_U, C_GV, C_END = 0, 512, 1024, 1536, 2048, 2176, 2688, 3200


def _vmem_full():
    return pl.BlockSpec(memory_space=pltpu.VMEM)


def _params(sem):
    return pltpu.CompilerParams(dimension_semantics=sem, vmem_limit_bytes=VMEM_LIMIT)


def _layer_norm(x, g, b):
    mu = jnp.mean(x, axis=-1, keepdims=True)
    xc = x - mu
    var = jnp.mean(xc * xc, axis=-1, keepdims=True)
    return xc * lax.rsqrt(var + LN_EPS) * g + b


def _low_half(rows):
    return lax.broadcasted_iota(I32, (rows, LANES), 1) < HEAD_DIM


def _mod_kernel(c_ref, w_ref, b_ref, o_ref):
    c = c_ref[...]
    s = c * jax.nn.sigmoid(c)
    o_ref[...] = jnp.dot(s.astype(BF16), w_ref[...], preferred_element_type=F32) + b_ref[...]


def _modulation(c, w_ada, b_ada):
    rows, d = c.shape
    n = w_ada.shape[1]
    tn = n // 4
    return pl.pallas_call(
        _mod_kernel,
        grid=(4,),
        in_specs=[pl.BlockSpec((rows, d), lambda j: (0, 0)),
                  pl.BlockSpec((d, tn), lambda j: (0, j)),
                  pl.BlockSpec((1, tn), lambda j: (0, j))],
        out_specs=pl.BlockSpec((rows, tn), lambda j: (0, j)),
        out_shape=jax.ShapeDtypeStruct((rows, n), F32),
        compiler_params=_params(("arbitrary",)),
        name="modulation",
    )(c, w_ada, b_ada)


def _proj_kernel(x_ref, sc_ref, sh_ref, w_ref, rope_ref, lng_ref, lnb_ref,
                 q_ref, kf_ref, vf_ref, kb_ref, vb_ref, qi_ref, kw_ref, u_ref, vn_ref):
    tm = x_ref.shape[0]
    h = x_ref[...] * (1.0 + sc_ref[...]) + sh_ref[...]
    hb = h.astype(BF16)
    cos = rope_ref[:, 0:LANES]
    sin_a = rope_ref[:, LANES:2 * LANES]
    sin_b = rope_ref[:, 2 * LANES:3 * LANES]

    def rope(zc):
        return zc * cos + pltpu.roll(zc, LANES - 8, 1) * sin_a + pltpu.roll(zc, 8, 1) * sin_b

    def mm(c0, c1):
        return jnp.dot(hb, w_ref[:, c0:c1], preferred_element_type=F32)

    zq = mm(C_Q, C_K)
    zk = mm(C_K, C_V)
    zv = mm(C_V, C_QI)
    zqi = mm(C_QI, C_KW)
    for j in range(4):
        sl = slice(j * LANES, (j + 1) * LANES)
        q_ref[:, sl] = (rope(zq[:, sl]) * Q_SCALE).astype(BF16)
        kr = rope(zk[:, sl])
        kf_ref[:, sl] = kr
        kb_ref[:, sl] = kr.astype(BF16)
        qi_ref[:, sl] = (rope(zqi[:, sl]) * (IDX_DIM ** -0.5)).astype(BF16)
    vf_ref[...] = zv
    vb_ref[...] = zv.astype(BF16)
    zkw = mm(C_KW, C_U)
    kw_ref[...] = jnp.where(_low_half(tm), rope(zkw), zkw * (IDX_HEADS ** -0.5))
    u_ref[...] = jax.nn.gelu(mm(C_U, C_GV))
    gv = jax.nn.gelu(mm(C_GV, C_END))
    vn_ref[...] = _layer_norm(gv, lng_ref[...], lnb_ref[...])


def _project(x, sc, sh, w_in_r, rope_tab, ln_g, ln_b, tm):
    rows, d = x.shape
    mod_rows = sc.shape[0]
    tmm = tm if mod_rows == rows else 1
    mod_map = (lambda i: (i, 0)) if mod_rows == rows else (lambda i: (0, 0))
    row = lambda w: pl.BlockSpec((tm, w), lambda i: (i, 0))
    const = lambda a: pl.BlockSpec(a.shape, lambda i: (0, 0))
    dh = 512
    out_shape = (jax.ShapeDtypeStruct((rows, dh), BF16),
                 jax.ShapeDtypeStruct((rows, dh), F32),
                 jax.ShapeDtypeStruct((rows, dh), F32),
                 jax.ShapeDtypeStruct((rows, dh), BF16),
                 jax.ShapeDtypeStruct((rows, dh), BF16),
                 jax.ShapeDtypeStruct((rows, dh), BF16),
                 jax.ShapeDtypeStruct((rows, LANES), F32),
                 jax.ShapeDtypeStruct((rows, dh), F32),
                 jax.ShapeDtypeStruct((rows, dh), F32))
    return pl.pallas_call(
        _proj_kernel,
        grid=(rows // tm,),
        in_specs=[row(d),
                  pl.BlockSpec((tmm, d), mod_map), pl.BlockSpec((tmm, d), mod_map),
                  _vmem_full(), row(3 * LANES), const(ln_g), const(ln_b)],
        out_specs=[row(dh), row(dh), row(dh), row(dh), row(dh), row(dh), row(LANES), row(dh), row(dh)],
        out_shape=out_shape,
        compiler_params=_params(("arbitrary",)),
        name="projection",
    )(x, sc, sh, w_in_r, rope_tab, ln_g, ln_b)


def _ordered_key(score):
    bits = lax.bitcast_convert_type(score, I32)
    return bits ^ ((bits >> 31) & 0x7FFFFFFF)


COUNT_UNROLL = 16


def _count(keys_ref, n_chunks, rows, pred):
    def body(g, acc):
        for u in range(COUNT_UNROLL):
            c = g * COUNT_UNROLL + u
            acc = acc + jnp.where(pred(keys_ref[c], c), 1.0, 0.0)
        return acc
    acc = lax.fori_loop(0, n_chunks // COUNT_UNROLL, body, jnp.zeros((rows, LANES), F32))
    return jnp.sum(acc, axis=1, keepdims=True)


I16 = jnp.int16
I16_MIN = -(2 ** 15)


def _count16(half_ref, n_chunks, rows, pred):
    one, zero = jnp.ones((rows, LANES), I16), jnp.zeros((rows, LANES), I16)

    def body(g, acc):
        for u in range(COUNT_UNROLL):
            acc = acc + jnp.where(pred(half_ref[g * COUNT_UNROLL + u]), one, zero)
        return acc
    acc = lax.fori_loop(0, n_chunks // COUNT_UNROLL, body, zero)
    return jnp.dot(acc.astype(F32).astype(BF16), jnp.ones((LANES, LANES), BF16), preferred_element_type=F32)


def _bisect16(half_ref, n_chunks, rows, need, cnt0):
    def bit_pass(i, carry):
        t, cnt_t = carry
        cand = t + jnp.left_shift(jnp.int32(1), 15 - i)
        cand_h = cand.astype(I16)
        cnt = _count16(half_ref, n_chunks, rows, lambda h: h >= cand_h)
        keep = cnt >= need
        return jnp.where(keep, cand, t), jnp.where(keep, cnt, cnt_t)

    return lax.fori_loop(0, 16, bit_pass, (jnp.full((rows, LANES), I16_MIN, I32), cnt0))


def _topk_threshold(keys_ref, half_ref, n_chunks, rows, n_sel, idx_bits):
    n_sel_f = jnp.full((rows, LANES), float(n_sel), F32)
    everything = jnp.full((rows, LANES), 2.0 ** 30, F32)
    t_hi, cnt_hi = _bisect16(half_ref, n_chunks, rows, n_sel_f, everything)
    t_hi_h = t_hi.astype(I16)
    above = _count16(half_ref, n_chunks, rows, lambda h: h > t_hi_h)

    def low_halves(g, carry):
        for u in range(COUNT_UNROLL):
            c = g * COUNT_UNROLL + u
            low = ((keys_ref[c] & 0xFFFF) - 2 ** 15).astype(I16)
            half_ref[c] = jnp.where(half_ref[c] == t_hi_h, low, jnp.full((rows, LANES), I16_MIN, I16))
        return carry
    lax.fori_loop(0, n_chunks // COUNT_UNROLL, low_halves, 0)
    t_lo, cnt_lo = _bisect16(half_ref, n_chunks, rows, n_sel_f - above, cnt_hi - above)

    t_b = t_hi * 65536 + (t_lo + 2 ** 15)
    tied_b = jnp.logical_and(above + cnt_lo > n_sel, t_b > INT_MIN)
    any_tied = jnp.max(jnp.where(tied_b, 1.0, 0.0)) > 0.0

    @pl.when(any_tied)
    def _():
        tied = tied_b[:, 0:1]
        cnt_gt = _count(keys_ref, n_chunks, rows, lambda k, c: k > t_b)
        need = n_sel - cnt_gt
        lane = lax.broadcasted_iota(I32, (rows, LANES), 1)

        def index_pass(i, j):
            cand = j + jnp.left_shift(jnp.int32(1), idx_bits - 1 - i)
            cand_b = jnp.broadcast_to(cand, (rows, LANES))
            cnt = _count(keys_ref, n_chunks, rows,
                         lambda k, c: jnp.where(k == t_b, c * LANES + lane, cand_b) < cand_b)
            return jnp.where(cnt < need, cand, j)

        j = lax.fori_loop(0, idx_bits, index_pass, jnp.zeros((rows, 1), I32))
        j_b = jnp.broadcast_to(jnp.where(tied, j, jnp.int32(2 ** 31 - 1)), (rows, LANES))

        def drop(c, carry):
            k = keys_ref[c]
            over = jnp.where(k == t_b, c * LANES + lane, j_b) > j_b
            keys_ref[c] = jnp.where(over, k - 1, k)
            return carry
        lax.fori_loop(0, n_chunks, drop, 0)

    return jnp.maximum(t_b, INT_MIN + 1)


TILE16 = 16
TILES = LANES // TILE16


def _row16(row):
    return jnp.broadcast_to(row, (TILE16, LANES)).astype(I16)


def _count_t(ref, n_chunks, pred):
    one, zero = jnp.ones((TILE16, LANES), I16), jnp.zeros((TILE16, LANES), I16)

    def body(g, accs):
        accs = list(accs)
        for u in range(COUNT_UNROLL):
            c = g * COUNT_UNROLL + u
            for s in range(TILES):
                tile = ref[c, s * TILE16:(s + 1) * TILE16, :]
                accs[s % 4] = accs[s % 4] + jnp.where(pred(tile, c, s), one, zero)
        return tuple(accs)
    if isinstance(n_chunks, int):
        assert n_chunks <= COUNT_UNROLL
        accs = (zero,) * 4
        for c in range(n_chunks):
            accs = list(accs)
            for s in range(TILES):
                accs[s % 4] = accs[s % 4] + jnp.where(pred(ref[c, s * TILE16:(s + 1) * TILE16, :], c, s), one, zero)
    else:
        accs = lax.fori_loop(0, n_chunks // COUNT_UNROLL, body, (zero,) * 4)
    tot = (accs[0].astype(F32) + accs[1].astype(F32)) + (accs[2].astype(F32) + accs[3].astype(F32))
    return jnp.sum(tot, axis=0, keepdims=True)


def _bisect_t(ref, n_chunks, need, cnt0):
    def bit_pass(i, carry):
        t, cnt_t = carry
        cand = t + jnp.left_shift(jnp.int32(1), 15 - i)
        cand16 = _row16(cand)
        cnt = _count_t(ref, n_chunks, lambda tile, c, s: tile >= cand16)
        keep = cnt >= need
        return jnp.where(keep, cand, t), jnp.where(keep, cnt, cnt_t)

    return lax.fori_loop(0, 16, bit_pass, (jnp.full((1, LANES), I16_MIN, I32), cnt0))


LOW_LIST = 3


def _bucket_lists(hi_ref, lo_ref, list_ref, n_chunks, t_hi16):
    floor = jnp.full((TILE16, LANES), I16_MIN, I16)
    one, zero = jnp.ones((TILE16, LANES), I16), jnp.zeros((TILE16, LANES), I16)
    unroll = 4

    def body(g, carry):
        lists, over = list(carry[0]), carry[1]
        for u in range(unroll):
            c = g * unroll + u
            for s in range(TILES):
                sl = slice(s * TILE16, (s + 1) * TILE16)
                x = jnp.where(hi_ref[c, sl, :] == t_hi16, lo_ref[c, sl, :], floor)
                for k in range(LOW_LIST):
                    cur = lists[s * LOW_LIST + k]
                    up = x > cur
                    lists[s * LOW_LIST + k] = jnp.where(up, x, cur)
                    x = jnp.where(up, cur, x)
                over = jnp.where(x > floor, one, over)
        return tuple(lists), over

    lists, over = lax.fori_loop(0, n_chunks // unroll, body, ((floor,) * (TILES * LOW_LIST), zero))
    for s in range(TILES):
        for k in range(LOW_LIST):
            list_ref[k, s * TILE16:(s + 1) * TILE16, :] = lists[s * LOW_LIST + k]
    return jnp.max(over.astype(F32), axis=0, keepdims=True)


def _topk_threshold_t(hi_ref, lo_ref, work_ref, n_chunks, n_sel, idx_bits):
    n_sel_f = jnp.full((1, LANES), float(n_sel), F32)
    everything = jnp.full((1, LANES), 2.0 ** 30, F32)
    t_hi, cnt_hi = _bisect_t(hi_ref, n_chunks, n_sel_f, everything)
    t_hi16 = _row16(t_hi)
    above = _count_t(hi_ref, n_chunks, lambda tile, c, s: tile > t_hi16)

    def low_halves(g, carry):
        for u in range(COUNT_UNROLL):
            c = g * COUNT_UNROLL + u
            for s in range(TILES):
                sl = slice(s * TILE16, (s + 1) * TILE16)
                work_ref[c, sl, :] = jnp.where(hi_ref[c, sl, :] == t_hi16, lo_ref[c, sl, :],
                                               jnp.full((TILE16, LANES), I16_MIN, I16))
        return carry

    need_lo, cnt0_lo = n_sel_f - above, cnt_hi - above
    overflowed = _bucket_lists(hi_ref, lo_ref, work_ref, n_chunks, t_hi16)

    def search_lists():
        return _bisect_t(work_ref, LOW_LIST, need_lo, cnt0_lo)

    def search_all():
        lax.fori_loop(0, n_chunks // COUNT_UNROLL, low_halves, 0)
        return _bisect_t(work_ref, n_chunks, need_lo, cnt0_lo)

    t_lo, cnt_lo = lax.cond(jnp.max(overflowed) > 0.0, search_all, search_lists)

    at_floor = jnp.logical_and(t_hi == I16_MIN, t_lo == I16_MIN)
    tied = jnp.logical_and(above + cnt_lo > n_sel, jnp.logical_not(at_floor))
    any_tied = jnp.max(jnp.where(tied, 1.0, 0.0)) > 0.0

    @pl.when(any_tied)
    def _():
        lax.fori_loop(0, n_chunks // COUNT_UNROLL, low_halves, 0)
        t_lo16 = _row16(t_lo)
        greater = above + _count_t(work_ref, n_chunks, lambda tile, c, s: tile > t_lo16)
        need = n_sel - greater
        pos = lax.broadcasted_iota(I32, (TILE16, LANES), 0)

        def index16(c, s):
            return (c * LANES + s * TILE16 + pos).astype(I16)

        def is_tied(c, s):
            sl = slice(s * TILE16, (s + 1) * TILE16)
            return jnp.logical_and(hi_ref[c, sl, :] == t_hi16, lo_ref[c, sl, :] == t_lo16)

        def index_pass(i, j):
            cand = j + jnp.left_shift(jnp.int32(1), idx_bits - 1 - i)
            cand16 = _row16(cand)
            cnt = _count_t(work_ref, n_chunks,
                           lambda tile, c, s: jnp.logical_and(is_tied(c, s), index16(c, s) < cand16))
            return jnp.where(cnt < need, cand, j)

        j = lax.fori_loop(0, idx_bits, index_pass, jnp.zeros((1, LANES), I32))
        j16 = _row16(jnp.where(tied, j, jnp.int32(2 ** 15 - 1)))
        floor16 = jnp.full((TILE16, LANES), I16_MIN, I16)

        def drop(c, carry):
            for s in range(TILES):
                sl = slice(s * TILE16, (s + 1) * TILE16)
                over = jnp.logical_and(is_tied(c, s), index16(c, s) > j16)
                hi_ref[c, sl, :] = jnp.where(over, floor16, hi_ref[c, sl, :])
                lo_ref[c, sl, :] = jnp.where(over, floor16, lo_ref[c, sl, :])
            return carry
        lax.fori_loop(0, n_chunks, drop, 0)

    return t_hi, jnp.where(at_floor, I16_MIN + 1, t_lo)


def _prompt_attn_kernel(q_ref, qi_ref, kw_ref, kt_ref, v_ref, kit_ref, o_ref,
                        hi_ref, lo_ref, work_ref, raw_ref, bias_ref, qm_ref, qim_ref, wb_ref, m_ref, l_ref, mblk_ref,
                        alpha_ref, acc_ref, s_ref, p_ref, *, tq, tkb, n_sel, idx_bits):
    qb = pl.program_id(0)
    q0 = qb * tq
    n_kb = 2 * ((q0 + tq - 1) // (2 * tkb) + 1)
    cpb = tkb // LANES
    low = _low_half(tq)
    lane = lax.broadcasted_iota(I32, (tq, LANES), 1)
    zero_b = jnp.zeros((tq, LANES), BF16)

    for j in range(4):
        sl = slice(j * LANES, (j + 1) * LANES)
        qc = q_ref[:, sl]
        qm_ref[2 * j] = jnp.where(low, qc, zero_b)
        qm_ref[2 * j + 1] = jnp.where(low, zero_b, qc)
        qic = qi_ref[:, sl]
        qim_ref[2 * j] = jnp.where(low, qic, zero_b)
        qim_ref[2 * j + 1] = jnp.where(low, zero_b, qic)
    kw = kw_ref[...]
    for h in range(IDX_HEADS):
        w_h = jnp.sum(jnp.where(lane == IDX_DIM + h, kw, 0.0), axis=1, keepdims=True)
        wb_ref[h] = jnp.broadcast_to(w_h, (tq, LANES))

    q_pos = q0 + lax.broadcasted_iota(I32, (tq, LANES), 0)

    def score_matmuls(kb, slot):
        kit = jnp.concatenate([kit_ref[kb], kit_ref[kb]], axis=0)
        acc = [jnp.zeros((tq, LANES), F32) for _ in range(cpb)]
        for h in range(IDX_HEADS):
            s = jnp.dot(qim_ref[h], kit, preferred_element_type=F32)
            w_h = wb_ref[h]
            for c in range(cpb):
                acc[c] = acc[c] + jnp.maximum(s[:, c * LANES:(c + 1) * LANES], 0.0) * w_h
        for c in range(cpb):
            raw_ref[slot, :, c * LANES:(c + 1) * LANES] = acc[c]

    def score_keys(kb, slot):
        for c in range(cpb):
            key_pos = kb * tkb + c * LANES + lane
            key = jnp.where(key_pos <= q_pos, _ordered_key(raw_ref[slot, :, c * LANES:(c + 1) * LANES]), INT_MIN)
            key_t = key.T
            hi_ref[kb * cpb + c] = (key_t >> 16).astype(I16)
            lo_ref[kb * cpb + c] = ((key_t & 0xFFFF) - 2 ** 15).astype(I16)

    def score_pair(i, carry):
        score_matmuls(2 * i + 1, 1)
        score_keys(2 * i, 0)
        score_matmuls(jnp.minimum(2 * i + 2, n_kb - 1), 0)
        score_keys(2 * i + 1, 1)
        return carry

    score_matmuls(0, 0)
    lax.fori_loop(0, n_kb // 2, score_pair, 0)

    n_chunks = n_kb * cpb
    pad_chunks = COUNT_UNROLL - 2 * cpb
    for c in range(pad_chunks):
        hi_ref[n_chunks + c] = jnp.full((LANES, tq), I16_MIN, I16)
        lo_ref[n_chunks + c] = jnp.full((LANES, tq), I16_MIN, I16)
    n_padded = (n_chunks + pad_chunks) // COUNT_UNROLL * COUNT_UNROLL
    t_hi, t_lo = _topk_threshold_t(hi_ref, lo_ref, work_ref, n_padded, n_sel, idx_bits)
    t_hi_c = jnp.broadcast_to(t_hi, (LANES, tq)).astype(I16)
    t_lo_c = jnp.broadcast_to(t_lo, (LANES, tq)).astype(I16)
    zero_c, neg_c = jnp.zeros((LANES, tq), BF16), jnp.full((LANES, tq), NEG, BF16)

    m_ref[...] = jnp.full(m_ref.shape, NEG, F32)
    l_ref[...] = jnp.zeros(l_ref.shape, F32)
    acc_ref[...] = jnp.zeros(acc_ref.shape, F32)
    n_heads = qm_ref.shape[0]

    def attend_logits(kb, slot):
        for c in range(cpb):
            hi_c, lo_c = hi_ref[kb * cpb + c], lo_ref[kb * cpb + c]
            sel = jnp.logical_or(hi_c > t_hi_c, jnp.logical_and(hi_c == t_hi_c, lo_c >= t_lo_c))
            bias_ref[:, c * LANES:(c + 1) * LANES] = jnp.where(sel, zero_c, neg_c).astype(F32).T
        for h in range(n_heads):
            kt2 = kt_ref[kb, (h // 2) * LANES:(h // 2 + 1) * LANES, :]
            s = jnp.dot(qm_ref[h], kt2, preferred_element_type=F32) + bias_ref[...]
            s_ref[slot, h] = s
            cm = s[:, 0:LANES]
            for c in range(1, cpb):
                cm = jnp.maximum(cm, s[:, c * LANES:(c + 1) * LANES])
            m_prev = m_ref[h]
            m_new = jnp.maximum(m_prev, jnp.max(cm, axis=1, keepdims=True))
            alpha_ref[slot, h] = jnp.exp2(m_prev - m_new)
            mblk_ref[slot, h] = m_new
            m_ref[h] = m_new

    def attend_values(kb, slot):
        for h in range(n_heads):
            m_blk = mblk_ref[slot, h]
            p_sum = None
            for c in range(cpb):
                sl = slice(c * LANES, (c + 1) * LANES)
                p = jnp.exp2(s_ref[slot, h, :, sl] - m_blk)
                p_ref[h, :, sl] = p.astype(BF16)
                p_sum = p if p_sum is None else p_sum + p
            l_ref[h] = alpha_ref[slot, h] * l_ref[h] + p_sum
        for j in range(n_heads // 2):
            v2 = v_ref[kb, :, j * LANES:(j + 1) * LANES]
            pv_e = jnp.dot(p_ref[2 * j], v2, preferred_element_type=F32)
            pv_o = jnp.dot(p_ref[2 * j + 1], v2, preferred_element_type=F32)
            a = acc_ref[j]
            acc_ref[j] = jnp.where(low, alpha_ref[slot, 2 * j] * a + pv_e, alpha_ref[slot, 2 * j + 1] * a + pv_o)

    def attend_pair(i, carry):
        attend_logits(2 * i + 1, 1)
        attend_values(2 * i, 0)
        attend_logits(jnp.minimum(2 * i + 2, n_kb - 1), 0)
        attend_values(2 * i + 1, 1)
        return carry

    attend_logits(0, 0)
    lax.fori_loop(0, n_kb // 2, attend_pair, 0)

    for j in range(n_heads // 2):
        l_e = jnp.sum(l_ref[2 * j], axis=1, keepdims=True)
        l_o = jnp.sum(l_ref[2 * j + 1], axis=1, keepdims=True)
        inv = jnp.where(low, 1.0 / l_e, 1.0 / l_o)
        o_ref[:, j * LANES:(j + 1) * LANES] = (acc_ref[j] * inv).astype(o_ref.dtype)


def _prompt_attention(q_bf, qi_bf, kw, k_bf, v_bf, n_sel):
    t, dh = q_bf.shape
    tq, tkb = 128, 512
    assert t % (2 * tkb) == 0 and t % tq == 0, "key blocks are processed in pairs"
    assert COUNT_UNROLL % (2 * tkb // LANES) == 0 and tq == LANES and t < 2 ** 15, "16-bit key indices"
    n_key_chunks = t // LANES + COUNT_UNROLL - 2 * tkb // LANES
    nb = t // tkb
    n_heads = dh // HEAD_DIM
    kt3 = k_bf.reshape(nb, tkb, dh).transpose(0, 2, 1)
    v3 = v_bf.reshape(nb, tkb, dh)
    kit = kw[:, :IDX_DIM].astype(BF16).reshape(nb, tkb, IDX_DIM).transpose(0, 2, 1)
    kern = functools.partial(_prompt_attn_kernel, tq=tq, tkb=tkb, n_sel=n_sel,
                             idx_bits=max(1, (t - 1).bit_length()))
    row = lambda w: pl.BlockSpec((tq, w), lambda i: (i, 0))
    return pl.pallas_call(
        kern,
        grid=(t // tq,),
        in_specs=[row(dh), row(dh), row(LANES), _vmem_full(), _vmem_full(), _vmem_full()],
        out_specs=row(dh),
        out_shape=jax.ShapeDtypeStruct((t, dh), BF16),
        scratch_shapes=[pltpu.VMEM((n_key_chunks, LANES, tq), I16),
                        pltpu.VMEM((n_key_chunks, LANES, tq), I16),
                        pltpu.VMEM((n_key_chunks, LANES, tq), I16),
                        pltpu.VMEM((2, tq, tkb), F32),
                        pltpu.VMEM((tq, tkb), F32),
                        pltpu.VMEM((n_heads, tq, LANES), BF16),
                        pltpu.VMEM((IDX_HEADS, tq, LANES), BF16),
                        pltpu.VMEM((IDX_HEADS, tq, LANES), F32),
                        pltpu.VMEM((n_heads, tq, LANES), F32),
                        pltpu.VMEM((n_heads, tq, LANES), F32),
                        pltpu.VMEM((2, n_heads, tq, LANES), F32),
                        pltpu.VMEM((2, n_heads, tq, LANES), F32),
                        pltpu.VMEM((n_heads // 2, tq, LANES), F32),
                        pltpu.VMEM((2, n_heads, tq, tkb), F32),
                        pltpu.VMEM((n_heads, tq, tkb), BF16)],
        compiler_params=_params(("arbitrary",)),
        name="prompt_attention",
    )(q_bf, qi_bf, kw, kt3, v3, kit)


def _gmlp_gate(u, vn, wp_ref, gb_ref, low):
    cols = []
    for j in range(4):
        sl = slice(j * LANES, (j + 1) * LANES)
        col = vn[:, sl]
        rhs = jnp.concatenate([jnp.where(low, col, 0.0), jnp.where(low, 0.0, col)], axis=0).astype(BF16)
        mixed = jnp.dot(wp_ref[j], rhs, preferred_element_type=F32) + gb_ref[:, sl]
        cols.append(u[:, sl] * mixed)
    return jnp.concatenate(cols, axis=1)


def _prompt_mix_kernel(x_ref, att_ref, u_ref, vn_ref, g1_ref, sc2_ref, sh2_ref, g2_ref,
                       wp_ref, gb_ref, wout_ref, ln1g_ref, ln1b_ref,
                       wup_ref, cw_ref, cb_ref, wdown_ref, ln2g_ref, ln2b_ref,
                       y_ref, conv_ref, gbuf_ref, *, alpha, ffc):
    tm = x_ref.shape[0]
    d_att = att_ref.shape[1]
    d_ff = cw_ref.shape[1]
    low = _low_half(CHUNK)

    @pl.when(pl.program_id(0) == 0)
    def _():
        gbuf_ref[0:8, :] = jnp.zeros((8, d_ff), F32)

    gm = jnp.concatenate(
        [_gmlp_gate(u_ref[c * CHUNK:(c + 1) * CHUNK, :], vn_ref[c * CHUNK:(c + 1) * CHUNK, :], wp_ref, gb_ref, low)
         for c in range(tm // CHUNK)], axis=0)
    mix = (jnp.dot(att_ref[...], wout_ref[0:d_att, :], preferred_element_type=F32)
           + jnp.dot(gm.astype(BF16), wout_ref[d_att:, :], preferred_element_type=F32))
    x1 = _layer_norm(alpha * x_ref[...] + g1_ref[...] * mix, ln1g_ref[...], ln1b_ref[...])
    h2b = (x1 * (1.0 + sc2_ref[...]) + sh2_ref[...]).astype(BF16)

    f = jnp.zeros((tm, x_ref.shape[1]), F32)
    for c in range(d_ff // ffc):
        sl = slice(c * ffc, (c + 1) * ffc)
        g = jnp.dot(h2b, wup_ref[:, sl], preferred_element_type=F32)
        a = jnp.dot(h2b, wup_ref[:, d_ff + c * ffc:d_ff + (c + 1) * ffc], preferred_element_type=F32)
        gbuf_ref[8:8 + tm, sl] = g
        g_conv = (cb_ref[:, sl] + cw_ref[0:1, sl] * gbuf_ref[6:6 + tm, sl]
                  + cw_ref[1:2, sl] * gbuf_ref[7:7 + tm, sl] + cw_ref[2:3, sl] * g)
        act = g_conv * jax.nn.sigmoid(g_conv) * a
        f = f + jnp.dot(act.astype(BF16), wdown_ref[sl, :], preferred_element_type=F32)
        gbuf_ref[6:8, sl] = gbuf_ref[tm + 6:tm + 8, sl]
    y_ref[...] = _layer_norm(alpha * x1 + g2_ref[...] * f, ln2g_ref[...], ln2b_ref[...])
    conv_ref[...] = gbuf_ref[6:8, :]


def _prompt_mix_ffn(x, att, u, vn, g1, sc2, sh2, g2, wpair, gbias, w_out, ln1g, ln1b,
                    w_up, conv_w, conv_b, w_down, ln2g, ln2b, alpha):
    t, d = x.shape
    tm = 256
    d_ff = conv_w.shape[1]
    row = lambda w: pl.BlockSpec((tm, w), lambda i: (i, 0))
    const = lambda a: pl.BlockSpec(a.shape, lambda i: (0,) * a.ndim)
    kern = functools.partial(_prompt_mix_kernel, alpha=alpha, ffc=256)
    return pl.pallas_call(
        kern,
        grid=(t // tm,),
        in_specs=[row(d), row(att.shape[1]), row(u.shape[1]), row(vn.shape[1]),
                  const(g1), const(sc2), const(sh2), const(g2),
                  _vmem_full(), const(gbias), _vmem_full(), const(ln1g), const(ln1b),
                  _vmem_full(), const(conv_w), const(conv_b), _vmem_full(), const(ln2g), const(ln2b)],
        out_specs=[row(d), pl.BlockSpec((CONV_W - 1, d_ff), lambda i: (0, 0))],
        out_shape=(jax.ShapeDtypeStruct((t, d), F32), jax.ShapeDtypeStruct((CONV_W - 1, d_ff), F32)),
        scratch_shapes=[pltpu.VMEM((tm + 8, d_ff), F32)],
        compiler_params=_params(("arbitrary",)),
        name="prompt_mix_ffn",
    )(x, att, u, vn, g1, sc2, sh2, g2, wpair, gbias, w_out, ln1g, ln1b,
      w_up, conv_w, conv_b, w_down, ln2g, ln2b)


def _sample_score_kernel(pt_ref, q8_ref, w8_ref, *refs, pages):
    page_refs, o_ref = refs[:pages], refs[pages]
    q8 = q8_ref[0]
    w8 = w8_ref[0]
    for i in range(pages):
        kit = page_refs[i][0].astype(BF16)
        s = jnp.dot(q8, kit, preferred_element_type=F32)
        sc = jnp.sum(jnp.maximum(s, 0.0) * w8, axis=0, keepdims=True)
        o_ref[0, :, i * kit.shape[1]:(i + 1) * kit.shape[1]] = sc


def _sample_scores(page_table, q8, w8, cache_idx_kt, pages):
    db, n_pages = page_table.shape
    _, di, page = cache_idx_kt.shape
    page_spec = lambda i: pl.BlockSpec((1, di, page), lambda b, g, pt: (pt[b, g * pages + i], 0, 0))
    grid_spec = pltpu.PrefetchScalarGridSpec(
        num_scalar_prefetch=1,
        grid=(db, n_pages // pages),
        in_specs=[pl.BlockSpec((1, IDX_HEADS, di), lambda b, g, pt: (b, 0, 0)),
                  pl.BlockSpec((1, IDX_HEADS, 1), lambda b, g, pt: (b, 0, 0))]
                 + [page_spec(i) for i in range(pages)],
        out_specs=pl.BlockSpec((1, 1, pages * page), lambda b, g, pt: (b, 0, g)),
    )
    return pl.pallas_call(
        functools.partial(_sample_score_kernel, pages=pages),
        grid_spec=grid_spec,
        out_shape=jax.ShapeDtypeStruct((db, 1, n_pages * page), F32),
        compiler_params=_params(("arbitrary", "arbitrary")),
        name="sample_scores",
    )(page_table, q8, w8, *([cache_idx_kt] * pages))


def _sample_select_kernel(sc_ref, qi_ref, kw_ref, bp_ref, bn_ref, keys_ref, half_ref, *, n_sel, idx_bits):
    db, past = sc_ref.shape
    n_past = past // LANES
    low = _low_half(db)
    lane = lax.broadcasted_iota(I32, (db, LANES), 1)
    kw = kw_ref[...]
    ki2 = jnp.where(low, kw, pltpu.roll(kw, HEAD_DIM, 1))
    s_new = jnp.zeros((db, 1), F32)
    for j in range(4):
        prod = qi_ref[:, j * LANES:(j + 1) * LANES] * ki2
        for e in range(2):
            h = 2 * j + e
            s_h = jnp.sum(jnp.where(low, prod, 0.0) if e == 0 else jnp.where(low, 0.0, prod), axis=1, keepdims=True)
            w_h = jnp.sum(jnp.where(lane == IDX_DIM + h, kw, 0.0), axis=1, keepdims=True)
            s_new = s_new + jnp.maximum(s_h, 0.0) * w_h
    def put(c, key):
        keys_ref[c] = key
        half_ref[c] = (key >> 16).astype(I16)

    for c in range(n_past):
        put(c, _ordered_key(sc_ref[:, c * LANES:(c + 1) * LANES]))
    put(n_past, jnp.where(lane == 0, _ordered_key(jnp.broadcast_to(s_new, (db, LANES))), INT_MIN))
    for c in range(n_past + 1, keys_ref.shape[0]):
        put(c, jnp.full((db, LANES), INT_MIN, I32))
    thr = _topk_threshold(keys_ref, half_ref, keys_ref.shape[0], db, n_sel, idx_bits)
    thr_b = jnp.broadcast_to(thr, (db, LANES))
    for c in range(n_past):
        bp_ref[:, c * LANES:(c + 1) * LANES] = jnp.where(keys_ref[c] >= thr_b, 0.0, NEG)
    bn_ref[...] = jnp.where(keys_ref[n_past] >= thr_b, 0.0, NEG)


def _sample_select(scores, qi_f, kw, n_sel):
    db, past = scores.shape
    kern = functools.partial(_sample_select_kernel, n_sel=n_sel, idx_bits=past.bit_length())
    n_chunks = -(-(past // LANES + 1) // COUNT_UNROLL) * COUNT_UNROLL
    return pl.pallas_call(
        kern,
        in_specs=[_vmem_full(), _vmem_full(), _vmem_full()],
        out_specs=[_vmem_full(), _vmem_full()],
        out_shape=(jax.ShapeDtypeStruct((db, past), F32), jax.ShapeDtypeStruct((db, LANES), F32)),
        scratch_shapes=[pltpu.VMEM((n_chunks, db, LANES), I32), pltpu.VMEM((n_chunks, db, LANES), I16)],
        compiler_params=pltpu.CompilerParams(vmem_limit_bytes=VMEM_LIMIT),
        name="sample_select",
    )(scores, qi_f, kw)


def _sample_attn_kernel(pt_ref, qd_ref, bp_ref, bn_ref, kn_ref, vn_ref, *refs, pages):
    k_refs, v_refs = refs[:pages], refs[pages:2 * pages]
    o_ref, m_ref, l_ref, acc_ref = refs[2 * pages:]
    g = pl.program_id(1)
    n_heads, dh = qd_ref.shape[1], qd_ref.shape[2]
    page = k_refs[0].shape[2]

    @pl.when(g == 0)
    def _():
        m_ref[...] = jnp.full(m_ref.shape, NEG, F32)
        l_ref[...] = jnp.zeros(l_ref.shape, F32)
        acc_ref[...] = jnp.zeros(acc_ref.shape, F32)

    qd = qd_ref[0]
    logits = []
    for i in range(pages):
        ktp = k_refs[i][0].astype(BF16)
        s = jnp.dot(qd, ktp, preferred_element_type=F32)
        logits.append(s + bp_ref[0, :, i * page:(i + 1) * page])
    s_all = jnp.concatenate(logits, axis=1)
    m_prev = m_ref[...]
    m_new = jnp.maximum(m_prev, jnp.max(s_all, axis=1, keepdims=True))
    alpha = jnp.exp2(m_prev - m_new)
    p = jnp.exp2(s_all - m_new)
    l_ref[...] = alpha * l_ref[...] + jnp.sum(p, axis=1, keepdims=True)
    m_ref[...] = m_new
    acc = alpha * acc_ref[...]
    for i in range(pages):
        vtp = v_refs[i][0].astype(BF16)
        acc = acc + lax.dot_general(p[:, i * page:(i + 1) * page].astype(BF16), vtp,
                                    (((1,), (1,)), ((), ())), preferred_element_type=F32)
    acc_ref[...] = acc

    @pl.when(g == pl.num_programs(1) - 1)
    def _():
        s_n = jnp.sum(qd.astype(F32) * kn_ref[0], axis=1, keepdims=True) + bn_ref[0][:, 0:1]
        m_p = m_ref[...]
        m_f = jnp.maximum(m_p, s_n)
        a_f = jnp.exp2(m_p - m_f)
        p_n = jnp.exp2(s_n - m_f)
        l_f = a_f * l_ref[...] + p_n
        full = (a_f * acc_ref[...] + p_n * vn_ref[0]) / l_f
        head_of_lane = lax.broadcasted_iota(I32, (n_heads, dh), 1) // HEAD_DIM
        head_of_row = lax.broadcasted_iota(I32, (n_heads, dh), 0)
        o_ref[0] = jnp.sum(jnp.where(head_of_lane == head_of_row, full, 0.0), axis=0, keepdims=True)


def _sample_attention(page_table, qd, bias_past, bias_new, k_new, v_new, cache_kt, cache_vt, pages):
    db, n_pages = page_table.shape
    _, dh, page = cache_kt.shape
    n_heads = qd.shape[1]
    page_spec = lambda i: pl.BlockSpec((1, dh, page), lambda b, g, pt: (pt[b, g * pages + i], 0, 0))
    per_b = lambda r, w: pl.BlockSpec((1, r, w), lambda b, g, pt: (b, 0, 0))
    grid_spec = pltpu.PrefetchScalarGridSpec(
        num_scalar_prefetch=1,
        grid=(db, n_pages // pages),
        in_specs=[per_b(n_heads, dh),
                  pl.BlockSpec((1, 1, pages * page), lambda b, g, pt: (b, 0, g)),
                  per_b(1, LANES), per_b(1, dh), per_b(1, dh)]
                 + [page_spec(i) for i in range(pages)] + [page_spec(i) for i in range(pages)],
        out_specs=per_b(1, dh),
        scratch_shapes=[pltpu.VMEM((n_heads, 1), F32), pltpu.VMEM((n_heads, 1), F32),
                        pltpu.VMEM((n_heads, dh), F32)],
    )
    return pl.pallas_call(
        functools.partial(_sample_attn_kernel, pages=pages),
        grid_spec=grid_spec,
        out_shape=jax.ShapeDtypeStruct((db, 1, dh), F32),
        compiler_params=_params(("arbitrary", "arbitrary")),
        name="sample_attention",
    )(page_table, qd, bias_past, bias_new, k_new, v_new, *([cache_kt] * pages), *([cache_vt] * pages))


def _sample_mix_kernel(x_ref, att_ref, u_ref, vn_ref, g1_ref, sc2_ref, sh2_ref, g2_ref,
                       gw0_ref, gb0_ref, wout_ref, ln1g_ref, ln1b_ref,
                       wup_ref, cw_ref, cb_ref, wdown_ref, ln2g_ref, ln2b_ref, s0_ref, s1_ref,
                       y_ref, gnew_ref, *, alpha, ffc):
    d_att = att_ref.shape[1]
    d_ff = cw_ref.shape[1]
    gm = u_ref[...] * (vn_ref[...] * gw0_ref[...] + gb0_ref[...])
    mix = (jnp.dot(att_ref[...].astype(BF16), wout_ref[0:d_att, :], preferred_element_type=F32)
           + jnp.dot(gm.astype(BF16), wout_ref[d_att:, :], preferred_element_type=F32))
    x1 = _layer_norm(alpha * x_ref[...] + g1_ref[...] * mix, ln1g_ref[...], ln1b_ref[...])
    h2b = (x1 * (1.0 + sc2_ref[...]) + sh2_ref[...]).astype(BF16)
    f = jnp.zeros(x_ref.shape, F32)
    for c in range(d_ff // ffc):
        sl = slice(c * ffc, (c + 1) * ffc)
        g = jnp.dot(h2b, wup_ref[:, sl], preferred_element_type=F32)
        a = jnp.dot(h2b, wup_ref[:, d_ff + c * ffc:d_ff + (c + 1) * ffc], preferred_element_type=F32)
        gnew_ref[:, sl] = g
        g_conv = (cb_ref[:, sl] + cw_ref[0:1, sl] * s0_ref[:, sl] + cw_ref[1:2, sl] * s1_ref[:, sl]
                  + cw_ref[2:3, sl] * g)
        act = g_conv * jax.nn.sigmoid(g_conv) * a
        f = f + jnp.dot(act.astype(BF16), wdown_ref[sl, :], preferred_element_type=F32)
    y_ref[...] = _layer_norm(alpha * x1 + g2_ref[...] * f, ln2g_ref[...], ln2b_ref[...])


def _sample_mix_ffn(x, att, u, vn, g1, sc2, sh2, g2, gw0, gb0, w_out, ln1g, ln1b,
                    w_up, conv_w, conv_b, w_down, ln2g, ln2b, s0, s1, alpha):
    db, d = x.shape
    d_ff = conv_w.shape[1]
    n_in = 21
    return pl.pallas_call(
        functools.partial(_sample_mix_kernel, alpha=alpha, ffc=256),
        in_specs=[_vmem_full()] * n_in,
        out_specs=[_vmem_full(), _vmem_full()],
        out_shape=(jax.ShapeDtypeStruct((db, d), F32), jax.ShapeDtypeStruct((db, d_ff), F32)),
        compiler_params=pltpu.CompilerParams(vmem_limit_bytes=VMEM_LIMIT),
        name="sample_mix_ffn",
    )(x, att, u, vn, g1, sc2, sh2, g2, gw0, gb0, w_out, ln1g, ln1b,
      w_up, conv_w, conv_b, w_down, ln2g, ln2b, s0, s1)


def _rope_table(pos):
    rot = HEAD_DIM // ROPE_FRAC
    half = rot // 2
    inv = ROPE_THETA ** (-jnp.arange(half, dtype=F32) / half)
    ang = pos.astype(F32)[:, None] * inv[None, :]
    cos, sin = jnp.cos(ang), jnp.sin(ang)
    n = pos.shape[0]
    rest = HEAD_DIM - rot
    c64 = jnp.concatenate([cos, cos, jnp.ones((n, rest), F32)], axis=1)
    a64 = jnp.concatenate([-sin, jnp.zeros((n, rest + half), F32)], axis=1)
    b64 = jnp.concatenate([jnp.zeros((n, half), F32), sin, jnp.zeros((n, rest), F32)], axis=1)
    return jnp.concatenate([c64, c64, a64, a64, b64, b64], axis=1)


def _repack_w_in(w_in):
    d = w_in.shape[0]
    o_ki = 4 * 512
    o_u = o_ki + IDX_DIM + IDX_HEADS
    pad = jnp.zeros((d, LANES - IDX_DIM - IDX_HEADS), w_in.dtype)
    return jnp.concatenate([w_in[:, :o_ki], w_in[:, o_ki:o_u], pad, w_in[:, o_u:]], axis=1).astype(BF16)


def _layer(xp, xs, cp, cs, cache_k, cache_v, cache_idx_k, state_conv, page_table,
           w_ada, b_ada, w_in, gm_ln_g, gm_ln_b, gm_ws, gm_bs, w_out, ln1_g, ln1_b,
           w_up, conv_w, conv_b, w_down, ln2_g, ln2_b, alpha):
    b, t, d = xp.shape
    db, ts, _ = xs.shape
    assert b == 1 and ts == 1, "kernels are specialised to one prompt sequence and one decode position"
    n_phys, page = cache_k.shape[0], cache_k.shape[1]
    n_pages = page_table.shape[1]
    past = n_pages * page
    dh = cache_k.shape[2] * cache_k.shape[3]
    n_heads = cache_k.shape[2]
    d_ff = conv_w.shape[1]
    n_groups = gm_ws.shape[0]
    row2 = lambda a: a.reshape(1, -1)

    c_all = jnp.concatenate([cp, cs], axis=0)
    pad_rows = (-c_all.shape[0]) % 8
    c_all = jnp.pad(c_all, ((0, pad_rows), (0, 0)))
    mod = _modulation(c_all, w_ada.astype(BF16), row2(b_ada))
    mp = [mod[0:1, i * d:(i + 1) * d] for i in range(6)]
    ms = [mod[1:1 + db, i * d:(i + 1) * d] for i in range(6)]

    w_in_r = _repack_w_in(w_in)
    w_out_b, w_up_b, w_down_b = w_out.astype(BF16), w_up.astype(BF16), w_down.astype(BF16)
    ln_g, ln_b = row2(gm_ln_g), row2(gm_ln_b)

    x2 = xp.reshape(t, d)
    q_bf, k_f, v_f, k_bf, v_bf, qi_bf, kw, u, vn = _project(
        x2, mp[1], mp[0], w_in_r, _rope_table(jnp.arange(t)), ln_g, ln_b, tm=256)
    n_sel = min(TOPK_MAX, t // 4)
    att = _prompt_attention(q_bf, qi_bf, kw, k_bf, v_bf, n_sel)

    w_causal = (gm_ws * jnp.tril(jnp.ones((CHUNK, CHUNK), gm_ws.dtype))).astype(BF16)
    wpair = jnp.concatenate([w_causal[0::2], w_causal[1::2]], axis=2)
    gbias = jnp.repeat(gm_bs.T, GM_GROUP_DIM, axis=1)
    y_p, conv_p = _prompt_mix_ffn(x2, att, u, vn, mp[2], mp[4], mp[3], mp[5], wpair, gbias, w_out_b,
                                  row2(ln1_g), row2(ln1_b), w_up_b, conv_w, row2(conv_b), w_down_b,
                                  row2(ln2_g), row2(ln2_b), alpha)

    xs2 = xs.reshape(db, d)
    pos_s = jnp.full((db,), past, I32)
    qs_bf, ks_f, vs_f, _, _, qis_bf, kws, us, vns = _project(
        xs2, ms[1], ms[0], w_in_r, _rope_table(pos_s), ln_g, ln_b, tm=db)
    pages = 16
    q8 = qis_bf.reshape(db, IDX_HEADS, IDX_DIM)
    w8 = kws[:, IDX_DIM:IDX_DIM + IDX_HEADS].reshape(db, IDX_HEADS, 1)
    cache_idx_kt = jnp.swapaxes(cache_idx_k, 1, 2)
    cache_kt = jnp.transpose(cache_k, (0, 2, 3, 1)).reshape(n_phys, dh, page)
    cache_vt = jnp.transpose(cache_v, (0, 2, 3, 1)).reshape(n_phys, dh, page)
    scores = _sample_scores(page_table, q8, w8, cache_idx_kt, pages).reshape(db, past)
    n_sel_s = min(TOPK_MAX, (past + ts) // 4)
    bias_past, bias_new = _sample_select(scores, qis_bf.astype(F32), kws, n_sel_s)
    head_mask = (jnp.arange(dh)[None, :] // HEAD_DIM == jnp.arange(n_heads)[:, None])
    qd = jnp.where(head_mask[None], qs_bf[:, None, :], jnp.zeros((), BF16))
    att_s = _sample_attention(page_table, qd, bias_past.reshape(db, 1, past), bias_new.reshape(db, 1, LANES),
                              ks_f.reshape(db, 1, dh), vs_f.reshape(db, 1, dh),
                              cache_kt, cache_vt, pages)
    gw0 = row2(jnp.repeat(gm_ws[:, 0, 0], GM_GROUP_DIM))
    gb0 = row2(jnp.repeat(gm_bs[:, 0], GM_GROUP_DIM))
    y_s, g_new = _sample_mix_ffn(xs2, att_s.reshape(db, dh), us, vns, ms[2], ms[4], ms[3], ms[5], gw0, gb0,
                                 w_out_b, row2(ln1_g), row2(ln1_b), w_up_b, conv_w, row2(conv_b), w_down_b,
                                 row2(ln2_g), row2(ln2_b), state_conv[:, 0], state_conv[:, 1], alpha)
    conv_s = jnp.stack([state_conv[:, 1], g_new], axis=1)

    return (y_p.reshape(b, t, d), y_s.reshape(db, ts, d),
            k_f.reshape(b, t, n_heads, HEAD_DIM), v_f.reshape(b, t, n_heads, HEAD_DIM),
            kw[:, :IDX_DIM].reshape(b, t, IDX_DIM), conv_p.reshape(b, CONV_W - 1, d_ff),
            ks_f.reshape(db, ts, n_heads, HEAD_DIM), vs_f.reshape(db, ts, n_heads, HEAD_DIM),
            kws[:, :IDX_DIM].reshape(db, ts, IDX_DIM), vns.reshape(db, ts, -1), conv_s)


def kernel(x_prompt, x_sample, c_prompt, c_sample, cache_k, cache_v, cache_idx_k, state_conv, page_table,
           w_ada, b_ada, w_in, gm_ln_g, gm_ln_b, gm_ws, gm_bs, w_out, ln1_g, ln1_b,
           w_up, conv_w, conv_b, w_down, ln2_g, ln2_b):
    depth = w_ada.shape[0]
    alpha = (2.0 * depth) ** 0.25
    yp, ys = x_prompt, x_sample
    outs = [[] for _ in range(9)]
    for l in range(depth):
        res = _layer(yp, ys, c_prompt, c_sample, cache_k[l], cache_v[l], cache_idx_k[l], state_conv[l], page_table,
                     w_ada[l], b_ada[l], w_in[l], gm_ln_g[l], gm_ln_b[l], gm_ws[l], gm_bs[l], w_out[l],
                     ln1_g[l], ln1_b[l], w_up[l], conv_w[l], conv_b[l], w_down[l], ln2_g[l], ln2_b[l], alpha)
        yp, ys = res[0], res[1]
        for acc, r in zip(outs, res[2:]):
            acc.append(r)
    return (yp, ys) + tuple(jnp.stack(o) for o in outs)
```

```python
import functools

import jax
import jax.numpy as jnp
from jax import lax
from jax.experimental import pallas as pl
from jax.experimental.pallas import tpu as pltpu

F32 = jnp.float32
BF16 = jnp.bfloat16
I32 = jnp.int32

HEAD_DIM = 64
IDX_HEADS = 8
IDX_DIM = 64
TOPK_MAX = 256
ROPE_THETA = 500000.0
ROPE_FRAC = 4
GM_GROUP_DIM = 64
CHUNK = 128
CONV_W = 3
LN_EPS = 1e-5

Q_SCALE = HEAD_DIM ** -0.5 * 1.4426950408889634

LANES = 128
INT_MIN = -(2 ** 31)
NEG = -1e30
VMEM_LIMIT = 61 * 1024 * 1024

C_Q, C_K, C_V, C_QI, C_KW, C_U, C_GV, C_END = 0, 512, 1024, 1536, 2048, 2176, 2688, 3200


def _vmem_full():
    return pl.BlockSpec(memory_space=pltpu.VMEM)


def _params(sem):
    return pltpu.CompilerParams(dimension_semantics=sem, vmem_limit_bytes=VMEM_LIMIT)


def _layer_norm(x, g, b):
    mu = jnp.mean(x, axis=-1, keepdims=True)
    xc = x - mu
    var = jnp.mean(xc * xc, axis=-1, keepdims=True)
    return xc * lax.rsqrt(var + LN_EPS) * g + b


def _low_half(rows):
    return lax.broadcasted_iota(I32, (rows, LANES), 1) < HEAD_DIM


def _mod_kernel(c_ref, w_ref, b_ref, o_ref):
    c = c_ref[...]
    s = c * jax.nn.sigmoid(c)
    o_ref[...] = jnp.dot(s.astype(BF16), w_ref[...], preferred_element_type=F32) + b_ref[...]


def _modulation(c, w_ada, b_ada):
    rows, d = c.shape
    n = w_ada.shape[1]
    tn = n // 4
    return pl.pallas_call(
        _mod_kernel,
        grid=(4,),
        in_specs=[pl.BlockSpec((rows, d), lambda j: (0, 0)),
                  pl.BlockSpec((d, tn), lambda j: (0, j)),
                  pl.BlockSpec((1, tn), lambda j: (0, j))],
        out_specs=pl.BlockSpec((rows, tn), lambda j: (0, j)),
        out_shape=jax.ShapeDtypeStruct((rows, n), F32),
        compiler_params=_params(("arbitrary",)),
        name="modulation",
    )(c, w_ada, b_ada)


def _proj_kernel(x_ref, sc_ref, sh_ref, w_ref, rope_ref, lng_ref, lnb_ref,
                 q_ref, kf_ref, vf_ref, kb_ref, vb_ref, qi_ref, kw_ref, u_ref, vn_ref):
    tm = x_ref.shape[0]
    h = x_ref[...] * (1.0 + sc_ref[...]) + sh_ref[...]
    hb = h.astype(BF16)
    cos = rope_ref[:, 0:LANES]
    sin_a = rope_ref[:, LANES:2 * LANES]
    sin_b = rope_ref[:, 2 * LANES:3 * LANES]

    def rope(zc):
        return zc * cos + pltpu.roll(zc, LANES - 8, 1) * sin_a + pltpu.roll(zc, 8, 1) * sin_b

    def mm(c0, c1):
        return jnp.dot(hb, w_ref[:, c0:c1], preferred_element_type=F32)

    zq = mm(C_Q, C_K)
    zk = mm(C_K, C_V)
    zv = mm(C_V, C_QI)
    zqi = mm(C_QI, C_KW)
    for j in range(4):
        sl = slice(j * LANES, (j + 1) * LANES)
        q_ref[:, sl] = (rope(zq[:, sl]) * Q_SCALE).astype(BF16)
        kr = rope(zk[:, sl])
        kf_ref[:, sl] = kr
        kb_ref[:, sl] = kr.astype(BF16)
        qi_ref[:, sl] = (rope(zqi[:, sl]) * (IDX_DIM ** -0.5)).astype(BF16)
    vf_ref[...] = zv
    vb_ref[...] = zv.astype(BF16)
    zkw = mm(C_KW, C_U)
    kw_ref[...] = jnp.where(_low_half(tm), rope(zkw), zkw * (IDX_HEADS ** -0.5))
    u_ref[...] = jax.nn.gelu(mm(C_U, C_GV))
    gv = jax.nn.gelu(mm(C_GV, C_END))
    vn_ref[...] = _layer_norm(gv, lng_ref[...], lnb_ref[...])


def _project(x, sc, sh, w_in_r, rope_tab, ln_g, ln_b, tm):
    rows, d = x.shape
    mod_rows = sc.shape[0]
    tmm = tm if mod_rows == rows else 1
    mod_map = (lambda i: (i, 0)) if mod_rows == rows else (lambda i: (0, 0))
    row = lambda w: pl.BlockSpec((tm, w), lambda i: (i, 0))
    const = lambda a: pl.BlockSpec(a.shape, lambda i: (0, 0))
    dh = 512
    out_shape = (jax.ShapeDtypeStruct((rows, dh), BF16),
                 jax.ShapeDtypeStruct((rows, dh), F32),
                 jax.ShapeDtypeStruct((rows, dh), F32),
                 jax.ShapeDtypeStruct((rows, dh), BF16),
                 jax.ShapeDtypeStruct((rows, dh), BF16),
                 jax.ShapeDtypeStruct((rows, dh), BF16),
                 jax.ShapeDtypeStruct((rows, LANES), F32),
                 jax.ShapeDtypeStruct((rows, dh), F32),
                 jax.ShapeDtypeStruct((rows, dh), F32))
    return pl.pallas_call(
        _proj_kernel,
        grid=(rows // tm,),
        in_specs=[row(d),
                  pl.BlockSpec((tmm, d), mod_map), pl.BlockSpec((tmm, d), mod_map),
                  _vmem_full(), row(3 * LANES), const(ln_g), const(ln_b)],
        out_specs=[row(dh), row(dh), row(dh), row(dh), row(dh), row(dh), row(LANES), row(dh), row(dh)],
        out_shape=out_shape,
        compiler_params=_params(("arbitrary",)),
        name="projection",
    )(x, sc, sh, w_in_r, rope_tab, ln_g, ln_b)


def _ordered_key(score):
    bits = lax.bitcast_convert_type(score, I32)
    return bits ^ ((bits >> 31) & 0x7FFFFFFF)


COUNT_UNROLL = 8


def _count(keys_ref, n_chunks, rows, pred):
    def body(g, acc):
        for u in range(COUNT_UNROLL):
            c = g * COUNT_UNROLL + u
            acc = acc + jnp.where(pred(keys_ref[c], c), 1.0, 0.0)
        return acc
    acc = lax.fori_loop(0, n_chunks // COUNT_UNROLL, body, jnp.zeros((rows, LANES), F32))
    return jnp.sum(acc, axis=1, keepdims=True)


I16 = jnp.int16
I16_MIN = -(2 ** 15)


def _count16(half_ref, n_chunks, rows, pred):
    one, zero = jnp.ones((rows, LANES), I16), jnp.zeros((rows, LANES), I16)

    def body(g, acc):
        for u in range(COUNT_UNROLL):
            acc = acc + jnp.where(pred(half_ref[g * COUNT_UNROLL + u]), one, zero)
        return acc
    acc = lax.fori_loop(0, n_chunks // COUNT_UNROLL, body, zero)
    return jnp.dot(acc.astype(F32).astype(BF16), jnp.ones((LANES, LANES), BF16), preferred_element_type=F32)


def _bisect16(half_ref, n_chunks, rows, need, cnt0):
    def bit_pass(i, carry):
        t, cnt_t = carry
        cand = t + jnp.left_shift(jnp.int32(1), 15 - i)
        cand_h = cand.astype(I16)
        cnt = _count16(half_ref, n_chunks, rows, lambda h: h >= cand_h)
        keep = cnt >= need
        return jnp.where(keep, cand, t), jnp.where(keep, cnt, cnt_t)

    return lax.fori_loop(0, 16, bit_pass, (jnp.full((rows, LANES), I16_MIN, I32), cnt0))


def _topk_threshold(keys_ref, half_ref, n_chunks, rows, n_sel, idx_bits):
    n_sel_f = jnp.full((rows, LANES), float(n_sel), F32)
    everything = jnp.full((rows, LANES), 2.0 ** 30, F32)
    t_hi, cnt_hi = _bisect16(half_ref, n_chunks, rows, n_sel_f, everything)
    t_hi_h = t_hi.astype(I16)
    above = _count16(half_ref, n_chunks, rows, lambda h: h > t_hi_h)

    def low_halves(g, carry):
        for u in range(COUNT_UNROLL):
            c = g * COUNT_UNROLL + u
            low = ((keys_ref[c] & 0xFFFF) - 2 ** 15).astype(I16)
            half_ref[c] = jnp.where(half_ref[c] == t_hi_h, low, jnp.full((rows, LANES), I16_MIN, I16))
        return carry
    lax.fori_loop(0, n_chunks // COUNT_UNROLL, low_halves, 0)
    t_lo, cnt_lo = _bisect16(half_ref, n_chunks, rows, n_sel_f - above, cnt_hi - above)

    t_b = t_hi * 65536 + (t_lo + 2 ** 15)
    tied_b = jnp.logical_and(above + cnt_lo > n_sel, t_b > INT_MIN)
    any_tied = jnp.max(jnp.where(tied_b, 1.0, 0.0)) > 0.0

    @pl.when(any_tied)
    def _():
        tied = tied_b[:, 0:1]
        cnt_gt = _count(keys_ref, n_chunks, rows, lambda k, c: k > t_b)
        need = n_sel - cnt_gt
        lane = lax.broadcasted_iota(I32, (rows, LANES), 1)

        def index_pass(i, j):
            cand = j + jnp.left_shift(jnp.int32(1), idx_bits - 1 - i)
            cand_b = jnp.broadcast_to(cand, (rows, LANES))
            cnt = _count(keys_ref, n_chunks, rows,
                         lambda k, c: jnp.where(k == t_b, c * LANES + lane, cand_b) < cand_b)
            return jnp.where(cnt < need, cand, j)

        j = lax.fori_loop(0, idx_bits, index_pass, jnp.zeros((rows, 1), I32))
        j_b = jnp.broadcast_to(jnp.where(tied, j, jnp.int32(2 ** 31 - 1)), (rows, LANES))

        def drop(c, carry):
            k = keys_ref[c]
            over = jnp.where(k == t_b, c * LANES + lane, j_b) > j_b
            keys_ref[c] = jnp.where(over, k - 1, k)
            return carry
        lax.fori_loop(0, n_chunks, drop, 0)

    return jnp.maximum(t_b, INT_MIN + 1)


TILE16 = 16
TILES = LANES // TILE16


def _row16(row):
    return jnp.broadcast_to(row, (TILE16, LANES)).astype(I16)


def _count_t(ref, n_chunks, pred):
    one, zero = jnp.ones((TILE16, LANES), I16), jnp.zeros((TILE16, LANES), I16)

    def body(g, accs):
        accs = list(accs)
        for u in range(COUNT_UNROLL):
            c = g * COUNT_UNROLL + u
            for s in range(TILES):
                tile = ref[c, s * TILE16:(s + 1) * TILE16, :]
                accs[s % 4] = accs[s % 4] + jnp.where(pred(tile, c, s), one, zero)
        return tuple(accs)
    if isinstance(n_chunks, int):
        assert n_chunks <= COUNT_UNROLL
        accs = (zero,) * 4
        for c in range(n_chunks):
            accs = list(accs)
            for s in range(TILES):
                accs[s % 4] = accs[s % 4] + jnp.where(pred(ref[c, s * TILE16:(s + 1) * TILE16, :], c, s), one, zero)
    else:
        accs = lax.fori_loop(0, n_chunks // COUNT_UNROLL, body, (zero,) * 4)
    tot = (accs[0].astype(F32) + accs[1].astype(F32)) + (accs[2].astype(F32) + accs[3].astype(F32))
    return jnp.sum(tot, axis=0, keepdims=True)


def _bisect_t(ref, n_chunks, need, cnt0):
    def bit_pass(i, carry):
        t, cnt_t = carry
        cand = t + jnp.left_shift(jnp.int32(1), 15 - i)
        cand16 = _row16(cand)
        cnt = _count_t(ref, n_chunks, lambda tile, c, s: tile >= cand16)
        keep = cnt >= need
        return jnp.where(keep, cand, t), jnp.where(keep, cnt, cnt_t)

    return lax.fori_loop(0, 16, bit_pass, (jnp.full((1, LANES), I16_MIN, I32), cnt0))


LOW_LIST = 3


def _bucket_lists(hi_ref, lo_ref, list_ref, n_chunks, t_hi16):
    floor = jnp.full((TILE16, LANES), I16_MIN, I16)
    one, zero = jnp.ones((TILE16, LANES), I16), jnp.zeros((TILE16, LANES), I16)
    unroll = 4

    def body(g, carry):
        lists, over = list(carry[0]), carry[1]
        for u in range(unroll):
            c = g * unroll + u
            for s in range(TILES):
                sl = slice(s * TILE16, (s + 1) * TILE16)
                x = jnp.where(hi_ref[c, sl, :] == t_hi16, lo_ref[c, sl, :], floor)
                for k in range(LOW_LIST):
                    cur = lists[s * LOW_LIST + k]
                    up = x > cur
                    lists[s * LOW_LIST + k] = jnp.where(up, x, cur)
                    x = jnp.where(up, cur, x)
                over = jnp.where(x > floor, one, over)
        return tuple(lists), over

    lists, over = lax.fori_loop(0, n_chunks // unroll, body, ((floor,) * (TILES * LOW_LIST), zero))
    for s in range(TILES):
        for k in range(LOW_LIST):
            list_ref[k, s * TILE16:(s + 1) * TILE16, :] = lists[s * LOW_LIST + k]
    return jnp.max(over.astype(F32), axis=0, keepdims=True)


def _topk_threshold_t(hi_ref, lo_ref, work_ref, n_chunks, n_sel, idx_bits):
    n_sel_f = jnp.full((1, LANES), float(n_sel), F32)
    everything = jnp.full((1, LANES), 2.0 ** 30, F32)
    t_hi, cnt_hi = _bisect_t(hi_ref, n_chunks, n_sel_f, everything)
    t_hi16 = _row16(t_hi)
    above = _count_t(hi_ref, n_chunks, lambda tile, c, s: tile > t_hi16)

    def low_halves(g, carry):
        for u in range(COUNT_UNROLL):
            c = g * COUNT_UNROLL + u
            for s in range(TILES):
                sl = slice(s * TILE16, (s + 1) * TILE16)
                work_ref[c, sl, :] = jnp.where(hi_ref[c, sl, :] == t_hi16, lo_ref[c, sl, :],
                                               jnp.full((TILE16, LANES), I16_MIN, I16))
        return carry

    need_lo, cnt0_lo = n_sel_f - above, cnt_hi - above
    overflowed = _bucket_lists(hi_ref, lo_ref, work_ref, n_chunks, t_hi16)

    def search_lists():
        return _bisect_t(work_ref, LOW_LIST, need_lo, cnt0_lo)

    def search_all():
        lax.fori_loop(0, n_chunks // COUNT_UNROLL, low_halves, 0)
        return _bisect_t(work_ref, n_chunks, need_lo, cnt0_lo)

    t_lo, cnt_lo = lax.cond(jnp.max(overflowed) > 0.0, search_all, search_lists)

    at_floor = jnp.logical_and(t_hi == I16_MIN, t_lo == I16_MIN)
    tied = jnp.logical_and(above + cnt_lo > n_sel, jnp.logical_not(at_floor))
    any_tied = jnp.max(jnp.where(tied, 1.0, 0.0)) > 0.0

    @pl.when(any_tied)
    def _():
        lax.fori_loop(0, n_chunks // COUNT_UNROLL, low_halves, 0)
        t_lo16 = _row16(t_lo)
        greater = above + _count_t(work_ref, n_chunks, lambda tile, c, s: tile > t_lo16)
        need = n_sel - greater
        pos = lax.broadcasted_iota(I32, (TILE16, LANES), 0)

        def index16(c, s):
            return (c * LANES + s * TILE16 + pos).astype(I16)

        def is_tied(c, s):
            sl = slice(s * TILE16, (s + 1) * TILE16)
            return jnp.logical_and(hi_ref[c, sl, :] == t_hi16, lo_ref[c, sl, :] == t_lo16)

        def index_pass(i, j):
            cand = j + jnp.left_shift(jnp.int32(1), idx_bits - 1 - i)
            cand16 = _row16(cand)
            cnt = _count_t(work_ref, n_chunks,
                           lambda tile, c, s: jnp.logical_and(is_tied(c, s), index16(c, s) < cand16))
            return jnp.where(cnt < need, cand, j)

        j = lax.fori_loop(0, idx_bits, index_pass, jnp.zeros((1, LANES), I32))
        j16 = _row16(jnp.where(tied, j, jnp.int32(2 ** 15 - 1)))
        floor16 = jnp.full((TILE16, LANES), I16_MIN, I16)

        def drop(c, carry):
            for s in range(TILES):
                sl = slice(s * TILE16, (s + 1) * TILE16)
                over = jnp.logical_and(is_tied(c, s), index16(c, s) > j16)
                hi_ref[c, sl, :] = jnp.where(over, floor16, hi_ref[c, sl, :])
                lo_ref[c, sl, :] = jnp.where(over, floor16, lo_ref[c, sl, :])
            return carry
        lax.fori_loop(0, n_chunks, drop, 0)

    return t_hi, jnp.where(at_floor, I16_MIN + 1, t_lo)


def _prompt_attn_kernel(q_ref, qi_ref, kw_ref, kt_ref, v_ref, kit_ref, o_ref,
                        hi_ref, lo_ref, work_ref, raw_ref, bias_ref, qm_ref, qim_ref, wb_ref, m_ref, l_ref, mblk_ref,
                        alpha_ref, acc_ref, s_ref, p_ref, *, tq, tkb, n_sel, idx_bits):
    qb = pl.program_id(0)
    q0 = qb * tq
    n_kb = 2 * ((q0 + tq - 1) // (2 * tkb) + 1)
    cpb = tkb // LANES
    low = _low_half(tq)
    lane = lax.broadcasted_iota(I32, (tq, LANES), 1)
    zero_b = jnp.zeros((tq, LANES), BF16)

    for j in range(4):
        sl = slice(j * LANES, (j + 1) * LANES)
        qc = q_ref[:, sl]
        qm_ref[2 * j] = jnp.where(low, qc, zero_b)
        qm_ref[2 * j + 1] = jnp.where(low, zero_b, qc)
        qic = qi_ref[:, sl]
        qim_ref[2 * j] = jnp.where(low, qic, zero_b)
        qim_ref[2 * j + 1] = jnp.where(low, zero_b, qic)
    kw = kw_ref[...]
    for h in range(IDX_HEADS):
        w_h = jnp.sum(jnp.where(lane == IDX_DIM + h, kw, 0.0), axis=1, keepdims=True)
        wb_ref[h] = jnp.broadcast_to(w_h, (tq, LANES))

    q_pos = q0 + lax.broadcasted_iota(I32, (tq, LANES), 0)

    def score_matmuls(kb, slot):
        kit = jnp.concatenate([kit_ref[kb], kit_ref[kb]], axis=0)
        acc = [jnp.zeros((tq, LANES), F32) for _ in range(cpb)]
        for h in range(IDX_HEADS):
            s = jnp.dot(qim_ref[h], kit, preferred_element_type=F32)
            w_h = wb_ref[h]
            for c in range(cpb):
                acc[c] = acc[c] + jnp.maximum(s[:, c * LANES:(c + 1) * LANES], 0.0) * w_h
        for c in range(cpb):
            raw_ref[slot, :, c * LANES:(c + 1) * LANES] = acc[c]

    def score_keys(kb, slot):
        for c in range(cpb):
            key_pos = kb * tkb + c * LANES + lane
            key = jnp.where(key_pos <= q_pos, _ordered_key(raw_ref[slot, :, c * LANES:(c + 1) * LANES]), INT_MIN)
            key_t = key.T
            hi_ref[kb * cpb + c] = (key_t >> 16).astype(I16)
            lo_ref[kb * cpb + c] = ((key_t & 0xFFFF) - 2 ** 15).astype(I16)

    def score_pair(i, carry):
        score_matmuls(2 * i + 1, 1)
        score_keys(2 * i, 0)
        score_matmuls(jnp.minimum(2 * i + 2, n_kb - 1), 0)
        score_keys(2 * i + 1, 1)
        return carry

    score_matmuls(0, 0)
    lax.fori_loop(0, n_kb // 2, score_pair, 0)

    n_chunks = n_kb * cpb
    pad_chunks = COUNT_UNROLL - 2 * cpb
    for c in range(pad_chunks):
        hi_ref[n_chunks + c] = jnp.full((LANES, tq), I16_MIN, I16)
        lo_ref[n_chunks + c] = jnp.full((LANES, tq), I16_MIN, I16)
    n_padded = (n_chunks + pad_chunks) // COUNT_UNROLL * COUNT_UNROLL
    t_hi, t_lo = _topk_threshold_t(hi_ref, lo_ref, work_ref, n_padded, n_sel, idx_bits)
    t_hi_c = jnp.broadcast_to(t_hi, (LANES, tq)).astype(I16)
    t_lo_c = jnp.broadcast_to(t_lo, (LANES, tq)).astype(I16)
    zero_c, neg_c = jnp.zeros((LANES, tq), BF16), jnp.full((LANES, tq), NEG, BF16)

    m_ref[...] = jnp.full(m_ref.shape, NEG, F32)
    l_ref[...] = jnp.zeros(l_ref.shape, F32)
    acc_ref[...] = jnp.zeros(acc_ref.shape, F32)
    n_heads = qm_ref.shape[0]

    def attend_logits(kb, slot):
        for c in range(cpb):
            hi_c, lo_c = hi_ref[kb * cpb + c], lo_ref[kb * cpb + c]
            sel = jnp.logical_or(hi_c > t_hi_c, jnp.logical_and(hi_c == t_hi_c, lo_c >= t_lo_c))
            bias_ref[:, c * LANES:(c + 1) * LANES] = jnp.where(sel, zero_c, neg_c).astype(F32).T
        for h in range(n_heads):
            kt2 = kt_ref[kb, (h // 2) * LANES:(h // 2 + 1) * LANES, :]
            s = jnp.dot(qm_ref[h], kt2, preferred_element_type=F32) + bias_ref[...]
            s_ref[slot, h] = s
            cm = s[:, 0:LANES]
            for c in range(1, cpb):
                cm = jnp.maximum(cm, s[:, c * LANES:(c + 1) * LANES])
            m_prev = m_ref[h]
            m_new = jnp.maximum(m_prev, jnp.max(cm, axis=1, keepdims=True))
            alpha_ref[slot, h] = jnp.exp2(m_prev - m_new)
            mblk_ref[slot, h] = m_new
            m_ref[h] = m_new

    def attend_values(kb, slot):
        for h in range(n_heads):
            m_blk = mblk_ref[slot, h]
            p_sum = None
            for c in range(cpb):
                sl = slice(c * LANES, (c + 1) * LANES)
                p = jnp.exp2(s_ref[slot, h, :, sl] - m_blk)
                p_ref[h, :, sl] = p.astype(BF16)
                p_sum = p if p_sum is None else p_sum + p
            l_ref[h] = alpha_ref[slot, h] * l_ref[h] + p_sum
        for j in range(n_heads // 2):
            v2 = v_ref[kb, :, j * LANES:(j + 1) * LANES]
            pv_e = jnp.dot(p_ref[2 * j], v2, preferred_element_type=F32)
            pv_o = jnp.dot(p_ref[2 * j + 1], v2, preferred_element_type=F32)
            a = acc_ref[j]
            acc_ref[j] = jnp.where(low, alpha_ref[slot, 2 * j] * a + pv_e, alpha_ref[slot, 2 * j + 1] * a + pv_o)

    def attend_pair(i, carry):
        attend_logits(2 * i + 1, 1)
        attend_values(2 * i, 0)
        attend_logits(jnp.minimum(2 * i + 2, n_kb - 1), 0)
        attend_values(2 * i + 1, 1)
        return carry

    attend_logits(0, 0)
    lax.fori_loop(0, n_kb // 2, attend_pair, 0)

    for j in range(n_heads // 2):
        l_e = jnp.sum(l_ref[2 * j], axis=1, keepdims=True)
        l_o = jnp.sum(l_ref[2 * j + 1], axis=1, keepdims=True)
        inv = jnp.where(low, 1.0 / l_e, 1.0 / l_o)
        o_ref[:, j * LANES:(j + 1) * LANES] = (acc_ref[j] * inv).astype(o_ref.dtype)


def _prompt_attention(q_bf, qi_bf, kw, k_bf, v_bf, n_sel):
    t, dh = q_bf.shape
    tq, tkb = 128, 512
    assert t % (2 * tkb) == 0 and t % tq == 0, "key blocks are processed in pairs"
    assert COUNT_UNROLL % (2 * tkb // LANES) == 0 and tq == LANES and t < 2 ** 15, "16-bit key indices"
    n_key_chunks = t // LANES + COUNT_UNROLL - 2 * tkb // LANES
    nb = t // tkb
    n_heads = dh // HEAD_DIM
    kt3 = k_bf.reshape(nb, tkb, dh).transpose(0, 2, 1)
    v3 = v_bf.reshape(nb, tkb, dh)
    kit = kw[:, :IDX_DIM].astype(BF16).reshape(nb, tkb, IDX_DIM).transpose(0, 2, 1)
    kern = functools.partial(_prompt_attn_kernel, tq=tq, tkb=tkb, n_sel=n_sel,
                             idx_bits=max(1, (t - 1).bit_length()))
    row = lambda w: pl.BlockSpec((tq, w), lambda i: (i, 0))
    return pl.pallas_call(
        kern,
        grid=(t // tq,),
        in_specs=[row(dh), row(dh), row(LANES), _vmem_full(), _vmem_full(), _vmem_full()],
        out_specs=row(dh),
        out_shape=jax.ShapeDtypeStruct((t, dh), BF16),
        scratch_shapes=[pltpu.VMEM((n_key_chunks, LANES, tq), I16),
                        pltpu.VMEM((n_key_chunks, LANES, tq), I16),
                        pltpu.VMEM((n_key_chunks, LANES, tq), I16),
                        pltpu.VMEM((2, tq, tkb), F32),
                        pltpu.VMEM((tq, tkb), F32),
                        pltpu.VMEM((n_heads, tq, LANES), BF16),
                        pltpu.VMEM((IDX_HEADS, tq, LANES), BF16),
                        pltpu.VMEM((IDX_HEADS, tq, LANES), F32),
                        pltpu.VMEM((n_heads, tq, LANES), F32),
                        pltpu.VMEM((n_heads, tq, LANES), F32),
                        pltpu.VMEM((2, n_heads, tq, LANES), F32),
                        pltpu.VMEM((2, n_heads, tq, LANES), F32),
                        pltpu.VMEM((n_heads // 2, tq, LANES), F32),
                        pltpu.VMEM((2, n_heads, tq, tkb), F32),
                        pltpu.VMEM((n_heads, tq, tkb), BF16)],
        compiler_params=_params(("arbitrary",)),
        name="prompt_attention",
    )(q_bf, qi_bf, kw, kt3, v3, kit)


def _gmlp_gate(u, vn, wp_ref, gb_ref, low):
    cols = []
    for j in range(4):
        sl = slice(j * LANES, (j + 1) * LANES)
        col = vn[:, sl]
        rhs = jnp.concatenate([jnp.where(low, col, 0.0), jnp.where(low, 0.0, col)], axis=0).astype(BF16)
        mixed = jnp.dot(wp_ref[j], rhs, preferred_element_type=F32) + gb_ref[:, sl]
        cols.append(u[:, sl] * mixed)
    return jnp.concatenate(cols, axis=1)


def _prompt_mix_kernel(x_ref, att_ref, u_ref, vn_ref, g1_ref, sc2_ref, sh2_ref, g2_ref,
                       wp_ref, gb_ref, wout_ref, ln1g_ref, ln1b_ref,
                       wup_ref, cw_ref, cb_ref, wdown_ref, ln2g_ref, ln2b_ref,
                       y_ref, conv_ref, gbuf_ref, *, alpha, ffc):
    tm = x_ref.shape[0]
    d_att = att_ref.shape[1]
    d_ff = cw_ref.shape[1]
    low = _low_half(CHUNK)

    @pl.when(pl.program_id(0) == 0)
    def _():
        gbuf_ref[0:8, :] = jnp.zeros((8, d_ff), F32)

    gm = jnp.concatenate(
        [_gmlp_gate(u_ref[c * CHUNK:(c + 1) * CHUNK, :], vn_ref[c * CHUNK:(c + 1) * CHUNK, :], wp_ref, gb_ref, low)
         for c in range(tm // CHUNK)], axis=0)
    mix = (jnp.dot(att_ref[...], wout_ref[0:d_att, :], preferred_element_type=F32)
           + jnp.dot(gm.astype(BF16), wout_ref[d_att:, :], preferred_element_type=F32))
    x1 = _layer_norm(alpha * x_ref[...] + g1_ref[...] * mix, ln1g_ref[...], ln1b_ref[...])
    h2b = (x1 * (1.0 + sc2_ref[...]) + sh2_ref[...]).astype(BF16)

    f = jnp.zeros((tm, x_ref.shape[1]), F32)
    for c in range(d_ff // ffc):
        sl = slice(c * ffc, (c + 1) * ffc)
        g = jnp.dot(h2b, wup_ref[:, sl], preferred_element_type=F32)
        a = jnp.dot(h2b, wup_ref[:, d_ff + c * ffc:d_ff + (c + 1) * ffc], preferred_element_type=F32)
        gbuf_ref[8:8 + tm, sl] = g
        g_conv = (cb_ref[:, sl] + cw_ref[0:1, sl] * gbuf_ref[6:6 + tm, sl]
                  + cw_ref[1:2, sl] * gbuf_ref[7:7 + tm, sl] + cw_ref[2:3, sl] * g)
        act = g_conv * jax.nn.sigmoid(g_conv) * a
        f = f + jnp.dot(act.astype(BF16), wdown_ref[sl, :], preferred_element_type=F32)
        gbuf_ref[6:8, sl] = gbuf_ref[tm + 6:tm + 8, sl]
    y_ref[...] = _layer_norm(alpha * x1 + g2_ref[...] * f, ln2g_ref[...], ln2b_ref[...])
    conv_ref[...] = gbuf_ref[6:8, :]


def _prompt_mix_ffn(x, att, u, vn, g1, sc2, sh2, g2, wpair, gbias, w_out, ln1g, ln1b,
                    w_up, conv_w, conv_b, w_down, ln2g, ln2b, alpha):
    t, d = x.shape
    tm = 512
    d_ff = conv_w.shape[1]
    row = lambda w: pl.BlockSpec((tm, w), lambda i: (i, 0))
    const = lambda a: pl.BlockSpec(a.shape, lambda i: (0,) * a.ndim)
    kern = functools.partial(_prompt_mix_kernel, alpha=alpha, ffc=256)
    return pl.pallas_call(
        kern,
        grid=(t // tm,),
        in_specs=[row(d), row(att.shape[1]), row(u.shape[1]), row(vn.shape[1]),
                  const(g1), const(sc2), const(sh2), const(g2),
                  _vmem_full(), const(gbias), _vmem_full(), const(ln1g), const(ln1b),
                  _vmem_full(), const(conv_w), const(conv_b), _vmem_full(), const(ln2g), const(ln2b)],
        out_specs=[row(d), pl.BlockSpec((CONV_W - 1, d_ff), lambda i: (0, 0))],
        out_shape=(jax.ShapeDtypeStruct((t, d), F32), jax.ShapeDtypeStruct((CONV_W - 1, d_ff), F32)),
        scratch_shapes=[pltpu.VMEM((tm + 8, d_ff), F32)],
        compiler_params=_params(("arbitrary",)),
        name="prompt_mix_ffn",
    )(x, att, u, vn, g1, sc2, sh2, g2, wpair, gbias, w_out, ln1g, ln1b,
      w_up, conv_w, conv_b, w_down, ln2g, ln2b)


def _sample_score_kernel(pt_ref, q8_ref, w8_ref, *refs, pages):
    page_refs, o_ref = refs[:pages], refs[pages]
    q8 = q8_ref[0]
    w8 = w8_ref[0]
    for i in range(pages):
        kit = page_refs[i][0].astype(BF16)
        s = jnp.dot(q8, kit, preferred_element_type=F32)
        sc = jnp.sum(jnp.maximum(s, 0.0) * w8, axis=0, keepdims=True)
        o_ref[0, :, i * kit.shape[1]:(i + 1) * kit.shape[1]] = sc


def _sample_scores(page_table, q8, w8, cache_idx_kt, pages):
    db, n_pages = page_table.shape
    _, di, page = cache_idx_kt.shape
    page_spec = lambda i: pl.BlockSpec((1, di, page), lambda b, g, pt: (pt[b, g * pages + i], 0, 0))
    grid_spec = pltpu.PrefetchScalarGridSpec(
        num_scalar_prefetch=1,
        grid=(db, n_pages // pages),
        in_specs=[pl.BlockSpec((1, IDX_HEADS, di), lambda b, g, pt: (b, 0, 0)),
                  pl.BlockSpec((1, IDX_HEADS, 1), lambda b, g, pt: (b, 0, 0))]
                 + [page_spec(i) for i in range(pages)],
        out_specs=pl.BlockSpec((1, 1, pages * page), lambda b, g, pt: (b, 0, g)),
    )
    return pl.pallas_call(
        functools.partial(_sample_score_kernel, pages=pages),
        grid_spec=grid_spec,
        out_shape=jax.ShapeDtypeStruct((db, 1, n_pages * page), F32),
        compiler_params=_params(("arbitrary", "arbitrary")),
        name="sample_scores",
    )(page_table, q8, w8, *([cache_idx_kt] * pages))


def _sample_select_kernel(sc_ref, qi_ref, kw_ref, bp_ref, bn_ref, keys_ref, half_ref, *, n_sel, idx_bits):
    db, past = sc_ref.shape
    n_past = past // LANES
    low = _low_half(db)
    lane = lax.broadcasted_iota(I32, (db, LANES), 1)
    kw = kw_ref[...]
    ki2 = jnp.where(low, kw, pltpu.roll(kw, HEAD_DIM, 1))
    s_new = jnp.zeros((db, 1), F32)
    for j in range(4):
        prod = qi_ref[:, j * LANES:(j + 1) * LANES] * ki2
        for e in range(2):
            h = 2 * j + e
            s_h = jnp.sum(jnp.where(low, prod, 0.0) if e == 0 else jnp.where(low, 0.0, prod), axis=1, keepdims=True)
            w_h = jnp.sum(jnp.where(lane == IDX_DIM + h, kw, 0.0), axis=1, keepdims=True)
            s_new = s_new + jnp.maximum(s_h, 0.0) * w_h
    def put(c, key):
        keys_ref[c] = key
        half_ref[c] = (key >> 16).astype(I16)

    for c in range(n_past):
        put(c, _ordered_key(sc_ref[:, c * LANES:(c + 1) * LANES]))
    put(n_past, jnp.where(lane == 0, _ordered_key(jnp.broadcast_to(s_new, (db, LANES))), INT_MIN))
    for c in range(n_past + 1, keys_ref.shape[0]):
        put(c, jnp.full((db, LANES), INT_MIN, I32))
    thr = _topk_threshold(keys_ref, half_ref, keys_ref.shape[0], db, n_sel, idx_bits)
    thr_b = jnp.broadcast_to(thr, (db, LANES))
    for c in range(n_past):
        bp_ref[:, c * LANES:(c + 1) * LANES] = jnp.where(keys_ref[c] >= thr_b, 0.0, NEG)
    bn_ref[...] = jnp.where(keys_ref[n_past] >= thr_b, 0.0, NEG)


def _sample_select(scores, qi_f, kw, n_sel):
    db, past = scores.shape
    kern = functools.partial(_sample_select_kernel, n_sel=n_sel, idx_bits=past.bit_length())
    n_chunks = -(-(past // LANES + 1) // COUNT_UNROLL) * COUNT_UNROLL
    return pl.pallas_call(
        kern,
        in_specs=[_vmem_full(), _vmem_full(), _vmem_full()],
        out_specs=[_vmem_full(), _vmem_full()],
        out_shape=(jax.ShapeDtypeStruct((db, past), F32), jax.ShapeDtypeStruct((db, LANES), F32)),
        scratch_shapes=[pltpu.VMEM((n_chunks, db, LANES), I32), pltpu.VMEM((n_chunks, db, LANES), I16)],
        compiler_params=pltpu.CompilerParams(vmem_limit_bytes=VMEM_LIMIT),
        name="sample_select",
    )(scores, qi_f, kw)


def _sample_attn_kernel(pt_ref, qd_ref, bp_ref, bn_ref, kn_ref, vn_ref, *refs, pages):
    k_refs, v_refs = refs[:pages], refs[pages:2 * pages]
    o_ref, m_ref, l_ref, acc_ref = refs[2 * pages:]
    g = pl.program_id(1)
    n_heads, dh = qd_ref.shape[1], qd_ref.shape[2]
    page = k_refs[0].shape[2]

    @pl.when(g == 0)
    def _():
        m_ref[...] = jnp.full(m_ref.shape, NEG, F32)
        l_ref[...] = jnp.zeros(l_ref.shape, F32)
        acc_ref[...] = jnp.zeros(acc_ref.shape, F32)

    qd = qd_ref[0]
    logits = []
    for i in range(pages):
        ktp = k_refs[i][0].astype(BF16)
        s = jnp.dot(qd, ktp, preferred_element_type=F32)
        logits.append(s + bp_ref[0, :, i * page:(i + 1) * page])
    s_all = jnp.concatenate(logits, axis=1)
    m_prev = m_ref[...]
    m_new = jnp.maximum(m_prev, jnp.max(s_all, axis=1, keepdims=True))
    alpha = jnp.exp2(m_prev - m_new)
    p = jnp.exp2(s_all - m_new)
    l_ref[...] = alpha * l_ref[...] + jnp.sum(p, axis=1, keepdims=True)
    m_ref[...] = m_new
    acc = alpha * acc_ref[...]
    for i in range(pages):
        vtp = v_refs[i][0].astype(BF16)
        acc = acc + lax.dot_general(p[:, i * page:(i + 1) * page].astype(BF16), vtp,
                                    (((1,), (1,)), ((), ())), preferred_element_type=F32)
    acc_ref[...] = acc

    @pl.when(g == pl.num_programs(1) - 1)
    def _():
        s_n = jnp.sum(qd.astype(F32) * kn_ref[0], axis=1, keepdims=True) + bn_ref[0][:, 0:1]
        m_p = m_ref[...]
        m_f = jnp.maximum(m_p, s_n)
        a_f = jnp.exp2(m_p - m_f)
        p_n = jnp.exp2(s_n - m_f)
        l_f = a_f * l_ref[...] + p_n
        full = (a_f * acc_ref[...] + p_n * vn_ref[0]) / l_f
        head_of_lane = lax.broadcasted_iota(I32, (n_heads, dh), 1) // HEAD_DIM
        head_of_row = lax.broadcasted_iota(I32, (n_heads, dh), 0)
        o_ref[0] = jnp.sum(jnp.where(head_of_lane == head_of_row, full, 0.0), axis=0, keepdims=True)


def _sample_attention(page_table, qd, bias_past, bias_new, k_new, v_new, cache_kt, cache_vt, pages):
    db, n_pages = page_table.shape
    _, dh, page = cache_kt.shape
    n_heads = qd.shape[1]
    page_spec = lambda i: pl.BlockSpec((1, dh, page), lambda b, g, pt: (pt[b, g * pages + i], 0, 0))
    per_b = lambda r, w: pl.BlockSpec((1, r, w), lambda b, g, pt: (b, 0, 0))
    grid_spec = pltpu.PrefetchScalarGridSpec(
        num_scalar_prefetch=1,
        grid=(db, n_pages // pages),
        in_specs=[per_b(n_heads, dh),
                  pl.BlockSpec((1, 1, pages * page), lambda b, g, pt: (b, 0, g)),
                  per_b(1, LANES), per_b(1, dh), per_b(1, dh)]
                 + [page_spec(i) for i in range(pages)] + [page_spec(i) for i in range(pages)],
        out_specs=per_b(1, dh),
        scratch_shapes=[pltpu.VMEM((n_heads, 1), F32), pltpu.VMEM((n_heads, 1), F32),
                        pltpu.VMEM((n_heads, dh), F32)],
    )
    return pl.pallas_call(
        functools.partial(_sample_attn_kernel, pages=pages),
        grid_spec=grid_spec,
        out_shape=jax.ShapeDtypeStruct((db, 1, dh), F32),
        compiler_params=_params(("arbitrary", "arbitrary")),
        name="sample_attention",
    )(page_table, qd, bias_past, bias_new, k_new, v_new, *([cache_kt] * pages), *([cache_vt] * pages))


def _sample_mix_kernel(x_ref, att_ref, u_ref, vn_ref, g1_ref, sc2_ref, sh2_ref, g2_ref,
                       gw0_ref, gb0_ref, wout_ref, ln1g_ref, ln1b_ref,
                       wup_ref, cw_ref, cb_ref, wdown_ref, ln2g_ref, ln2b_ref, s0_ref, s1_ref,
                       y_ref, gnew_ref, *, alpha, ffc):
    d_att = att_ref.shape[1]
    d_ff = cw_ref.shape[1]
    gm = u_ref[...] * (vn_ref[...] * gw0_ref[...] + gb0_ref[...])
    mix = (jnp.dot(att_ref[...].astype(BF16), wout_ref[0:d_att, :], preferred_element_type=F32)
           + jnp.dot(gm.astype(BF16), wout_ref[d_att:, :], preferred_element_type=F32))
    x1 = _layer_norm(alpha * x_ref[...] + g1_ref[...] * mix, ln1g_ref[...], ln1b_ref[...])
    h2b = (x1 * (1.0 + sc2_ref[...]) + sh2_ref[...]).astype(BF16)
    f = jnp.zeros(x_ref.shape, F32)
    for c in range(d_ff // ffc):
        sl = slice(c * ffc, (c + 1) * ffc)
        g = jnp.dot(h2b, wup_ref[:, sl], preferred_element_type=F32)
        a = jnp.dot(h2b, wup_ref[:, d_ff + c * ffc:d_ff + (c + 1) * ffc], preferred_element_type=F32)
        gnew_ref[:, sl] = g
        g_conv = (cb_ref[:, sl] + cw_ref[0:1, sl] * s0_ref[:, sl] + cw_ref[1:2, sl] * s1_ref[:, sl]
                  + cw_ref[2:3, sl] * g)
        act = g_conv * jax.nn.sigmoid(g_conv) * a
        f = f + jnp.dot(act.astype(BF16), wdown_ref[sl, :], preferred_element_type=F32)
    y_ref[...] = _layer_norm(alpha * x1 + g2_ref[...] * f, ln2g_ref[...], ln2b_ref[...])


def _sample_mix_ffn(x, att, u, vn, g1, sc2, sh2, g2, gw0, gb0, w_out, ln1g, ln1b,
                    w_up, conv_w, conv_b, w_down, ln2g, ln2b, s0, s1, alpha):
    db, d = x.shape
    d_ff = conv_w.shape[1]
    n_in = 21
    return pl.pallas_call(
        functools.partial(_sample_mix_kernel, alpha=alpha, ffc=256),
        in_specs=[_vmem_full()] * n_in,
        out_specs=[_vmem_full(), _vmem_full()],
        out_shape=(jax.ShapeDtypeStruct((db, d), F32), jax.ShapeDtypeStruct((db, d_ff), F32)),
        compiler_params=pltpu.CompilerParams(vmem_limit_bytes=VMEM_LIMIT),
        name="sample_mix_ffn",
    )(x, att, u, vn, g1, sc2, sh2, g2, gw0, gb0, w_out, ln1g, ln1b,
      w_up, conv_w, conv_b, w_down, ln2g, ln2b, s0, s1)


def _rope_table(pos):
    rot = HEAD_DIM // ROPE_FRAC
    half = rot // 2
    inv = ROPE_THETA ** (-jnp.arange(half, dtype=F32) / half)
    d = jnp.arange(LANES) % HEAD_DIM
    ang = pos.astype(F32)[:, None] * inv[d % half][None, :]
    cos, sin = jnp.cos(ang), jnp.sin(ang)
    c = jnp.where(d < rot, cos, 1.0)
    a = jnp.where(d < half, -sin, 0.0)
    b = jnp.where(jnp.logical_and(d >= half, d < rot), sin, 0.0)
    return jnp.concatenate([c, a, b], axis=1)


def _repack_w_in(w_in):
    d = w_in.shape[0]
    o_ki = 4 * 512
    o_u = o_ki + IDX_DIM + IDX_HEADS
    pad = jnp.zeros((d, LANES - IDX_DIM - IDX_HEADS), w_in.dtype)
    return jnp.concatenate([w_in[:, :o_ki], w_in[:, o_ki:o_u], pad, w_in[:, o_u:]], axis=1).astype(BF16)


def _layer(xp, xs, cp, cs, cache_k, cache_v, cache_idx_k, state_conv, page_table,
           w_ada, b_ada, w_in, gm_ln_g, gm_ln_b, gm_ws, gm_bs, w_out, ln1_g, ln1_b,
           w_up, conv_w, conv_b, w_down, ln2_g, ln2_b, alpha):
    b, t, d = xp.shape
    db, ts, _ = xs.shape
    assert b == 1 and ts == 1, "kernels are specialised to one prompt sequence and one decode position"
    n_phys, page = cache_k.shape[0], cache_k.shape[1]
    n_pages = page_table.shape[1]
    past = n_pages * page
    dh = cache_k.shape[2] * cache_k.shape[3]
    n_heads = cache_k.shape[2]
    d_ff = conv_w.shape[1]
    n_groups = gm_ws.shape[0]
    row2 = lambda a: a.reshape(1, -1)

    c_all = jnp.concatenate([cp, cs], axis=0)
    pad_rows = (-c_all.shape[0]) % 8
    c_all = jnp.pad(c_all, ((0, pad_rows), (0, 0)))
    mod = _modulation(c_all, w_ada.astype(BF16), row2(b_ada))
    mp = [mod[0:1, i * d:(i + 1) * d] for i in range(6)]
    ms = [mod[1:1 + db, i * d:(i + 1) * d] for i in range(6)]

    w_in_r = _repack_w_in(w_in)
    w_out_b, w_up_b, w_down_b = w_out.astype(BF16), w_up.astype(BF16), w_down.astype(BF16)
    ln_g, ln_b = row2(gm_ln_g), row2(gm_ln_b)

    x2 = xp.reshape(t, d)
    q_bf, k_f, v_f, k_bf, v_bf, qi_bf, kw, u, vn = _project(
        x2, mp[1], mp[0], w_in_r, _rope_table(jnp.arange(t)), ln_g, ln_b, tm=512)
    n_sel = min(TOPK_MAX, t // 4)
    att = _prompt_attention(q_bf, qi_bf, kw, k_bf, v_bf, n_sel)

    w_causal = (gm_ws * jnp.tril(jnp.ones((CHUNK, CHUNK), gm_ws.dtype))).astype(BF16)
    wpair = jnp.concatenate([w_causal[0::2], w_causal[1::2]], axis=2)
    gbias = jnp.repeat(gm_bs.T, GM_GROUP_DIM, axis=1)
    y_p, conv_p = _prompt_mix_ffn(x2, att, u, vn, mp[2], mp[4], mp[3], mp[5], wpair, gbias, w_out_b,
                                  row2(ln1_g), row2(ln1_b), w_up_b, conv_w, row2(conv_b), w_down_b,
                                  row2(ln2_g), row2(ln2_b), alpha)

    xs2 = xs.reshape(db, d)
    pos_s = jnp.full((db,), past, I32)
    qs_bf, ks_f, vs_f, _, _, qis_bf, kws, us, vns = _project(
        xs2, ms[1], ms[0], w_in_r, _rope_table(pos_s), ln_g, ln_b, tm=db)
    pages = min(16, n_pages)
    assert n_pages % (4 * pages) == 0 or n_pages <= 4 * pages
    q8 =qis_bf.reshape(db, IDX_HEADS, IDX_DIM)
    w8 = kws[:, IDX_DIM:IDX_DIM + IDX_HEADS].reshape(db, IDX_HEADS, 1)
    cache_idx_kt = jnp.swapaxes(cache_idx_k, 1, 2)
    cache_kt = jnp.transpose(cache_k, (0, 2, 3, 1)).reshape(n_phys, dh, page)
    cache_vt = jnp.transpose(cache_v, (0, 2, 3, 1)).reshape(n_phys, dh, page)
    scores = _sample_scores(page_table, q8, w8, cache_idx_kt, min(4 * pages, n_pages)).reshape(db, past)
    n_sel_s = min(TOPK_MAX, (past + ts) // 4)
    bias_past, bias_new = _sample_select(scores, qis_bf.astype(F32), kws, n_sel_s)
    head_mask = (jnp.arange(dh)[None, :] // HEAD_DIM == jnp.arange(n_heads)[:, None])
    qd = jnp.where(head_mask[None], qs_bf[:, None, :], jnp.zeros((), BF16))
    att_s = _sample_attention(page_table, qd, bias_past.reshape(db, 1, past), bias_new.reshape(db, 1, LANES),
                              ks_f.reshape(db, 1, dh), vs_f.reshape(db, 1, dh),
                              cache_kt, cache_vt, pages)
    gw0 = row2(jnp.repeat(gm_ws[:, 0, 0], GM_GROUP_DIM))
    gb0 = row2(jnp.repeat(gm_bs[:, 0], GM_GROUP_DIM))
    y_s, g_new = _sample_mix_ffn(xs2, att_s.reshape(db, dh), us, vns, ms[2], ms[4], ms[3], ms[5], gw0, gb0,
                                 w_out_b, row2(ln1_g), row2(ln1_b), w_up_b, conv_w, row2(conv_b), w_down_b,
                                 row2(ln2_g), row2(ln2_b), state_conv[:, 0], state_conv[:, 1], alpha)
    conv_s = jnp.stack([state_conv[:, 1], g_new], axis=1)

    return (y_p.reshape(b, t, d), y_s.reshape(db, ts, d),
            k_f.reshape(b, t, n_heads, HEAD_DIM), v_f.reshape(b, t, n_heads, HEAD_DIM),
            kw[:, :IDX_DIM].reshape(b, t, IDX_DIM), conv_p.reshape(b, CONV_W - 1, d_ff),
            ks_f.reshape(db, ts, n_heads, HEAD_DIM), vs_f.reshape(db, ts, n_heads, HEAD_DIM),
            kws[:, :IDX_DIM].reshape(db, ts, IDX_DIM), vns.reshape(db, ts, -1), conv_s)


def kernel(x_prompt, x_sample, c_prompt, c_sample, cache_k, cache_v, cache_idx_k, state_conv, page_table,
           w_ada, b_ada, w_in, gm_ln_g, gm_ln_b, gm_ws, gm_bs, w_out, ln1_g, ln1_b,
           w_up, conv_w, conv_b, w_down, ln2_g, ln2_b):
    depth = w_ada.shape[0]
    alpha = (2.0 * depth) ** 0.25
    yp, ys = x_prompt, x_sample
    outs = [[] for _ in range(9)]
    for l in range(depth):
        res = _layer(yp, ys, c_prompt, c_sample, cache_k[l], cache_v[l], cache_idx_k[l], state_conv[l], page_table,
                     w_ada[l], b_ada[l], w_in[l], gm_ln_g[l], gm_ln_b[l], gm_ws[l], gm_bs[l], w_out[l],
                     ln1_g[l], ln1_b[l], w_up[l], conv_w[l], conv_b[l], w_down[l], ln2_g[l], ln2_b[l], alpha)
        yp, ys = res[0], res[1]
        for acc, r in zip(outs, res[2:]):
            acc.append(r)
    return (yp, ys) + tuple(jnp.stack(o) for o in outs)
```
